```python
import math
import jax, jax.numpy as jnp
from jax import lax
import numpy as np

D_MODEL = 1024
BATCH = 2
SEQ = 8192
DEPTH = 2

GRID_W = 64
CTX_LEN = 256
HEAD_DIM = 64
HG_HEADS = 4
HG_DK = 64
HG_DV = 64
HG_CHUNK = 64
GQA_HEADS = 6
GQA_KV_HEADS = 2
NA_HEADS = 6
NA_KH_MAX = 8
NA_KW = 16
ATTN_BLOCK = 128
ROPE_THETA = 10000.0
NORM_EPS = 1e-6
N_MOD = 6

HG_W = HG_HEADS * HG_DV
GQA_W = GQA_HEADS * HEAD_DIM
GQA_KV_W = GQA_KV_HEADS * HEAD_DIM
NA_W = NA_HEADS * HEAD_DIM
MIX_W = HG_W + GQA_W + NA_W
IN_SIZES = (HG_W, HG_W, HG_W, HG_W, HG_W, GQA_W, GQA_KV_W, GQA_KV_W, NA_W, NA_W, NA_W)
IN_W = sum(IN_SIZES)
FFN_HIDDEN = ((8 * D_MODEL + 3 * 256 - 1) // (3 * 256)) * 256
DN_ALPHA = (2 * DEPTH) ** 0.25
DN_BETA = (8 * DEPTH) ** -0.25

kernel_name = 'hybrid_hgrn2_gqa_natten_deepnorm_dit'


def rms_norm(x, g):
    x = x.astype(jnp.float32)
    return x * lax.rsqrt(jnp.mean(x * x, -1, keepdims=True) + NORM_EPS) * g


def layer_norm(x, g, b):
    x = x.astype(jnp.float32)
    xc = x - jnp.mean(x, -1, keepdims=True)
    return xc * lax.rsqrt(jnp.mean(xc * xc, -1, keepdims=True) + NORM_EPS) * g + b


def heads(a, n):
    return a.reshape(a.shape[:-1] + (n, a.shape[-1] // n))


def adaln(cv, w, b):
    return jax.nn.silu(cv) @ w + b


def modulate(x, shift, scale):
    return x * (1.0 + scale) + shift


def axial_rope_tables(n_tokens):
    t = jnp.arange(n_tokens, dtype=jnp.int32)
    row = (t // GRID_W).astype(jnp.float32)
    col = (t % GRID_W).astype(jnp.float32)
    n_pairs = HEAD_DIM // 4
    inv_freq = jnp.exp(-math.log(ROPE_THETA) * jnp.arange(n_pairs, dtype=jnp.float32) / n_pairs)
    ang = jnp.concatenate([row[:, None] * inv_freq, col[:, None] * inv_freq], -1)
    return jnp.cos(ang)[None, :, None, :], jnp.sin(ang)[None, :, None, :]


def apply_rope(x, cos, sin):
    x1, x2 = x[..., 0::2], x[..., 1::2]
    return jnp.stack([x1 * cos - x2 * sin, x1 * sin + x2 * cos], -1).reshape(x.shape)


def hgrn_lower_bounds(logits):
    p = jax.nn.softmax(logits.astype(jnp.float32), axis=0)
    return jnp.clip(jnp.cumsum(p, axis=0) - p[0], 0.0, 1.0)


def gla_chunk_scan(q, k, logf, v, s0, with_output):
    b_, n, h, _ = q.shape
    dv = v.shape[-1]
    nc = n // HG_CHUNK

    def to_chunks(a):
        return a.reshape(b_, nc, HG_CHUNK, h, a.shape[-1]).transpose(1, 0, 3, 2, 4)

    incl = jnp.tril(jnp.ones((HG_CHUNK, HG_CHUNK), bool))[:, :, None]

    def step(s, inp):
        qc, kc, gc, vc = inp
        cum = jnp.cumsum(gc, axis=2)
        tot = cum[:, :, -1:, :]
        s_new = jnp.exp(tot)[:, :, 0, :, None] * s + jnp.einsum('bhsk,bhsv->bhkv', kc * jnp.exp(tot - cum), vc)
        if not with_output:
            return s_new, None
        diff = cum[:, :, :, None, :] - cum[:, :, None, :, :]
        pair = jnp.where(incl, jnp.exp(jnp.minimum(diff, 0.0)), 0.0)
        att = jnp.einsum('bhtk,bhsk,bhtsk->bhts', qc, kc, pair)
        o = jnp.einsum('bhts,bhsv->bhtv', att, vc) + jnp.einsum('bhtk,bhkv->bhtv', qc * jnp.exp(cum), s)
        return s_new, o

    s_fin, o = lax.scan(step, s0, (to_chunks(q), to_chunks(k), to_chunks(logf), to_chunks(v)))
    if with_output:
        o = o.transpose(1, 0, 3, 2, 4).reshape(b_, n, h, dv)
    return o, s_fin


def hgrn2_mix(p_lat, p_ctx, lb, norm_g, need_ctx):
    def prep(p):
        q, f_fwd, f_bwd, i, g = p
        logf = [heads(jnp.log(lb[d] + (1.0 - lb[d]) * jax.nn.sigmoid(f)), HG_HEADS)
                for d, f in enumerate((f_fwd, f_bwd))]
        return heads(jax.nn.silu(q), HG_HEADS), heads(i, HG_HEADS), g, logf

    q_c, i_c, g_c, lf_c = prep(p_ctx)
    q_l, i_l, g_l, lf_l = prep(p_lat)
    s0 = jnp.zeros((q_l.shape[0], HG_HEADS, HG_DK, HG_DV), jnp.float32)
    rev = lambda a: jnp.flip(a, axis=1)
    o_cf, s_cf = gla_chunk_scan(q_c, -jnp.expm1(lf_c[0]), lf_c[0], i_c, s0, need_ctx)
    o_lf, _ = gla_chunk_scan(q_l, -jnp.expm1(lf_l[0]), lf_l[0], i_l, s_cf, True)
    o_cb, s_cb = gla_chunk_scan(rev(q_c), -jnp.expm1(rev(lf_c[1])), rev(lf_c[1]), rev(i_c), s0, need_ctx)
    o_lb, _ = gla_chunk_scan(rev(q_l), -jnp.expm1(rev(lf_l[1])), rev(lf_l[1]), rev(i_l), s_cb, True)

    def readout(o, g):
        return rms_norm(o, norm_g).reshape(g.shape) * jax.nn.silu(g)

    y_lat = readout(o_lf + rev(o_lb), g_l)
    y_ctx = readout(o_cf + rev(o_cb), g_c) if need_ctx else None
    return y_lat, y_ctx


def attend(q, k, v):
    s = jnp.einsum('bqkgd,bskd->bkgqs', q, k) * (q.shape[-1] ** -0.5)
    p = jax.nn.softmax(s.astype(jnp.float32), axis=-1)
    return jnp.einsum('bkgqs,bskd->bqkgd', p, v)


def blocked_attention(q, k, v):
    b_, n, h, dh = q.shape
    hkv = k.shape[2]
    nb = n // ATTN_BLOCK
    qb = q.reshape(b_, nb, ATTN_BLOCK, hkv, h // hkv, dh).transpose(1, 0, 2, 3, 4, 5)
    o = lax.map(lambda blk: attend(blk, k, v), qb)
    return o.transpose(1, 0, 2, 3, 4, 5).reshape(b_, n, h * dh)


def dense_attention(q, k, v):
    b_, n, h, dh = q.shape
    hkv = k.shape[2]
    return attend(q.reshape(b_, n, hkv, h // hkv, dh), k, v).reshape(b_, n, h * dh)


def gqa_mix(p_lat, p_ctx, q_norm, k_norm, cos, sin, need_ctx):
    def prep(p):
        q, k, v = p
        return (rms_norm(heads(q, GQA_HEADS), q_norm), rms_norm(heads(k, GQA_KV_HEADS), k_norm),
                heads(v, GQA_KV_HEADS))

    q_l, k_l, v_l = prep(p_lat)
    q_c, k_c, v_c = prep(p_ctx)
    q_l, k_l = apply_rope(q_l, cos, sin), apply_rope(k_l, cos, sin)
    k_all = jnp.concatenate([k_l, k_c], axis=1)
    v_all = jnp.concatenate([v_l, v_c], axis=1)
    y_lat = blocked_attention(q_l, k_all, v_all)
    y_ctx = dense_attention(q_c, k_c, v_c) if need_ctx else None
    return y_lat, y_ctx


def na_mix(p_lat, p_ctx, rpb, need_ctx):
    q_l, k_l, v_l = [heads(a, NA_HEADS) for a in p_lat]
    q_c, k_c, v_c = [heads(a, NA_HEADS) for a in p_ctx]
    b_, n = q_l.shape[:2]
    rows = n // GRID_W
    kh = min(NA_KH_MAX, rows)
    scale = HEAD_DIM ** -0.5
    cols = jnp.arange(GRID_W)
    col_idx = jnp.clip(cols - NA_KW // 2, 0, GRID_W - NA_KW)[:, None] + jnp.arange(NA_KW)
    dc = col_idx - cols[:, None] + (NA_KW - 1)
    n_loc = kh * NA_KW

    def row_block(r):
        row_idx = jnp.clip(r - kh // 2, 0, rows - kh) + jnp.arange(kh)
        idx = row_idx[None, :, None] * GRID_W + col_idx[:, None, :]
        kg, vg = k_l[:, idx], v_l[:, idx]
        qr = lax.dynamic_slice_in_dim(q_l, r * GRID_W, GRID_W, axis=1) * scale
        bias = rpb[:, (row_idx - r + NA_KH_MAX - 1)[None, :, None], dc[:, None, :]]
        s_loc = jnp.einsum('bchd,bcijhd->bhcij', qr, kg) + bias
        s_ctx = jnp.einsum('bchd,blhd->bhcl', qr, k_c)
        s = jnp.concatenate([s_loc.reshape(b_, NA_HEADS, GRID_W, n_loc), s_ctx], -1)
        p = jax.nn.softmax(s.astype(jnp.float32), axis=-1)
        p_loc = p[..., :n_loc].reshape(b_, NA_HEADS, GRID_W, kh, NA_KW)
        return (jnp.einsum('bhcij,bcijhd->bchd', p_loc, vg)
                + jnp.einsum('bhcl,blhd->bchd', p[..., n_loc:], v_c))

    o = lax.map(row_block, jnp.arange(rows))
    y_lat = o.transpose(1, 0, 2, 3, 4).reshape(b_, n, NA_W)
    y_ctx = dense_attention(q_c, k_c, v_c) if need_ctx else None
    return y_lat, y_ctx


def mixing_sublayer(h_lat, h_ctx, w_in, lb, hg_norm, q_norm, k_norm, rpb, w_out, cos, sin, need_ctx):
    split_at = np.cumsum(IN_SIZES)[:-1].tolist()
    p_lat = jnp.split((h_lat @ w_in).astype(jnp.float32), split_at, axis=-1)
    p_ctx = jnp.split((h_ctx @ w_in).astype(jnp.float32), split_at, axis=-1)
    hg_l, hg_c = hgrn2_mix(p_lat[0:5], p_ctx[0:5], lb, hg_norm, need_ctx)
    gq_l, gq_c = gqa_mix(p_lat[5:8], p_ctx[5:8], q_norm, k_norm, cos, sin, need_ctx)
    na_l, na_c = na_mix(p_lat[8:11], p_ctx[8:11], rpb, need_ctx)
    y_lat = jnp.concatenate([hg_l, gq_l, na_l], axis=-1) @ w_out
    y_ctx = jnp.concatenate([hg_c, gq_c, na_c], axis=-1) @ w_out if need_ctx else None
    return y_lat, y_ctx


def swiglu(h, w1, w2):
    gate, up = jnp.split(h @ w1, 2, axis=-1)
    return (jax.nn.silu(gate) * up) @ w2


def setup_inputs(seed: int = 0) -> dict:
    key = jax.random.key(seed)
    ks = jax.random.split(key, 17)
    f32 = jnp.float32

    def nrm(k, shape, s):
        return jax.random.normal(k, shape, f32) * s

    return {
        'x': nrm(ks[0], (BATCH, SEQ, D_MODEL), 1.0),
        'c': nrm(ks[1], (BATCH, D_MODEL), 1.0),
        'ctx': nrm(ks[2], (BATCH, CTX_LEN, D_MODEL), 1.0),
        'c_ctx': nrm(ks[3], (D_MODEL,), 1.0),
        'w_ada': nrm(ks[4], (DEPTH, D_MODEL, N_MOD * D_MODEL), 0.5 * D_MODEL ** -0.5),
        'b_ada': nrm(ks[5], (DEPTH, N_MOD * D_MODEL), 0.02),
        'w_in': nrm(ks[6], (DEPTH, D_MODEL, IN_W), D_MODEL ** -0.5),
        'hgrn_lb_logits': nrm(ks[7], (DEPTH, 2, HG_W), 1.0),
        'hgrn_norm': 1.0 + nrm(ks[8], (DEPTH, HG_DV), 0.02),
        'gqa_q_norm': 1.0 + nrm(ks[9], (DEPTH, HEAD_DIM), 0.02),
        'gqa_k_norm': 1.0 + nrm(ks[10], (DEPTH, HEAD_DIM), 0.02),
        'na_rpb': nrm(ks[11], (DEPTH, NA_HEADS, 2 * NA_KH_MAX - 1, 2 * NA_KW - 1), 0.1),
        'w_out': nrm(ks[12], (DEPTH, MIX_W, D_MODEL), DN_BETA * MIX_W ** -0.5),
        'w_ffn_in': nrm(ks[13], (DEPTH, D_MODEL, 2 * FFN_HIDDEN), D_MODEL ** -0.5),
        'w_ffn_out': nrm(ks[14], (DEPTH, FFN_HIDDEN, D_MODEL), DN_BETA * FFN_HIDDEN ** -0.5),
        'ln_g': 1.0 + nrm(ks[15], (DEPTH, 2, D_MODEL), 0.02),
        'ln_b': nrm(ks[16], (DEPTH, 2, D_MODEL), 0.02),
    }


def reference(x, c, ctx, c_ctx, w_ada, b_ada, w_in, hgrn_lb_logits, hgrn_norm, gqa_q_norm, gqa_k_norm,
              na_rpb, w_out, w_ffn_in, w_ffn_out, ln_g, ln_b):
    cos, sin = axial_rope_tables(x.shape[1])
    lbs = hgrn_lower_bounds(hgrn_lb_logits)
    x_lat, x_ctx = x, ctx
    for l in range(DEPTH):
        need_ctx = l < DEPTH - 1
        m_lat = jnp.split(adaln(c, w_ada[l], b_ada[l])[:, None, :], N_MOD, axis=-1)
        m_ctx = jnp.split(adaln(c_ctx, w_ada[l], b_ada[l]), N_MOD, axis=-1)
        y_lat, y_ctx = mixing_sublayer(
            modulate(x_lat, m_lat[0], m_lat[1]), modulate(x_ctx, m_ctx[0], m_ctx[1]),
            w_in[l], lbs[l], hgrn_norm[l], gqa_q_norm[l], gqa_k_norm[l], na_rpb[l], w_out[l],
            cos, sin, need_ctx)
        x_lat = layer_norm(DN_ALPHA * x_lat + m_lat[2] * y_lat, ln_g[l, 0], ln_b[l, 0])
        f_lat = swiglu(modulate(x_lat, m_lat[3], m_lat[4]), w_ffn_in[l], w_ffn_out[l])
        x_lat = layer_norm(DN_ALPHA * x_lat + m_lat[5] * f_lat, ln_g[l, 1], ln_b[l, 1])
        if need_ctx:
            x_ctx = layer_norm(DN_ALPHA * x_ctx + m_ctx[2] * y_ctx, ln_g[l, 0], ln_b[l, 0])
            f_ctx = swiglu(modulate(x_ctx, m_ctx[3], m_ctx[4]), w_ffn_in[l], w_ffn_out[l])
            x_ctx = layer_norm(DN_ALPHA * x_ctx + m_ctx[5] * f_ctx, ln_g[l, 1], ln_b[l, 1])
    return x_lat
```

```python
import functools
import math

import numpy as np
import jax
import jax.numpy as jnp
from jax import lax
from jax.experimental import pallas as pl
from jax.experimental.pallas import tpu as pltpu

F32 = jnp.float32
BF16 = jnp.bfloat16

D_MODEL = 1024
DEPTH = 2
GRID_W = 64
HEAD_DIM = 64
HG_HEADS = 4
GQA_HEADS = 6
GQA_KV_HEADS = 2
NA_HEADS = 6
NA_KH = 8
NA_KW = 16
ROPE_THETA = 10000.0
NORM_EPS = 1e-6
N_MOD = 6
HG_W = HG_HEADS * HEAD_DIM
GQA_W = GQA_HEADS * HEAD_DIM
GQA_KV_W = GQA_KV_HEADS * HEAD_DIM
NA_W = NA_HEADS * HEAD_DIM
FFN_HIDDEN = 2816
DN_ALPHA = (2 * DEPTH) ** 0.25

LANE = 128
SUB = 16
MOD_ROWS = 8
NEG = -1e30
VMEM_LIMIT = 56 * 1024 * 1024

C_HQ, C_FF, C_FB, C_HI, C_HGATE = 0, 256, 512, 768, 1024
C_GQ, C_GK, C_GV = 1280, 1664, 1792
C_NQ, C_NK, C_NV = 1920, 2304, 2688
C_GQP, C_GKP = 3072, 3456
IN_EXT = 3584
GQA_SLAB_HEADS = (0, 3, 1, 4, 2, 5)


def _cparams(sem):
    return pltpu.CompilerParams(dimension_semantics=sem, vmem_limit_bytes=VMEM_LIMIT)


def _sigmoid(z):
    return 1.0 / (1.0 + jnp.exp(-z))


def _dot(a, b):
    return jnp.dot(a, b, preferred_element_type=F32)


def _dot_nt(a, b):
    return lax.dot_general(a, b, (((1,), (1,)), ((), ())), preferred_element_type=F32)


def _mods_kernel(cv_ref, w_ref, b_ref, o_ref):
    cv = cv_ref[...]
    a = cv * _sigmoid(cv)
    o_ref[0] = jnp.dot(a, w_ref[0], preferred_element_type=F32, precision=lax.Precision.HIGHEST) + b_ref[0]


def _mods(cv, w_ada, b_ada):
    depth, d, nm = w_ada.shape
    tn = 1024
    return pl.pallas_call(
        _mods_kernel,
        grid=(depth, nm // tn),
        in_specs=[
            pl.BlockSpec((MOD_ROWS, d), lambda l, j: (0, 0)),
            pl.BlockSpec((1, d, tn), lambda l, j: (l, 0, j)),
            pl.BlockSpec((1, 1, tn), lambda l, j: (l, 0, j)),
        ],
        out_specs=pl.BlockSpec((1, MOD_ROWS, tn), lambda l, j: (l, 0, j)),
        out_shape=jax.ShapeDtypeStruct((depth, MOD_ROWS, nm), F32),
        compiler_params=_cparams(("arbitrary", "arbitrary")),
        name="mods",
    )(cv, w_ada, b_ada.reshape(depth, 1, nm))


def _inproj_kernel(layer, x_ref, sh_ref, sc_ref, w_ref, lbl_ref, cos_ref, sin_ref, gq_ref, gqp_ref, gk_ref, gkp_ref,
                   bd_ref, hq_ref, lf_ref, hi_ref, hg_ref, gq_o, gk_o, gv_o, nq_o, nk_o, nv_o):
    x = x_ref[0]
    h = (x * (1.0 + sc_ref[0]) + sh_ref[0]).astype(BF16)

    def proj(c0, width):
        return _dot(h, w_ref[:, c0:c0 + width])

    logits = [lbl_ref[d] for d in range(DEPTH)]
    mx = functools.reduce(jnp.maximum, logits)
    es = [jnp.exp(v - mx) for v in logits]
    inv = 1.0 / functools.reduce(lambda a, b: a + b, es)
    ps = [e * inv for e in es]
    lb = jnp.clip(functools.reduce(lambda a, b: a + b, ps[:layer + 1]) - ps[0], 0.0, 1.0)

    q = proj(C_HQ, HG_W)
    hq_ref[0] = q * _sigmoid(q)
    for d, c0 in enumerate((C_FF, C_FB)):
        z = proj(c0, HG_W)
        lbd = lb[d:d + 1, :]
        lf_ref[0, :, d * HG_W:(d + 1) * HG_W] = jnp.log(lbd + (1.0 - lbd) * _sigmoid(z))
    hi_ref[0] = proj(C_HI, HG_W)
    g = proj(C_HGATE, HG_W)
    hg_ref[0] = g * _sigmoid(g)

    cos = cos_ref[...]
    sin = sin_ref[...]
    bd = bd_ref[...]

    def normed_rope(c_main, c_part, gm_ref, gp_ref, out_ref, nslab):
        for s in range(nslab):
            pm = proj(c_main + s * LANE, LANE)
            pp = proj(c_part + s * LANE, LANE)
            ms = _dot(pm * pm, bd)
            r = lax.rsqrt(ms + NORM_EPS)
            gm = gm_ref[:, s * LANE:(s + 1) * LANE]
            gp = gp_ref[:, s * LANE:(s + 1) * LANE]
            out_ref[0, :, s * LANE:(s + 1) * LANE] = ((pm * r * gm) * cos + (pp * r * gp) * sin).astype(BF16)

    normed_rope(C_GQ, C_GQP, gq_ref, gqp_ref, gq_o, GQA_W // LANE)
    normed_rope(C_GK, C_GKP, gk_ref, gkp_ref, gk_o, GQA_KV_W // LANE)
    gv_o[0] = proj(C_GV, GQA_KV_W).astype(BF16)
    nq_o[0] = (proj(C_NQ, NA_W) * (HEAD_DIM ** -0.5)).astype(BF16)
    nk_o[0] = proj(C_NK, NA_W).astype(BF16)
    nv_o[0] = proj(C_NV, NA_W).astype(BF16)


def _inproj(layer, x, shift, scale, w_ext, lb_logits, cos, sin, gains, bd128, tm):
    b, t, d = x.shape
    gq, gqp, gk, gkp = gains
    row = lambda bi, i: (bi, i, 0)
    const2 = lambda bi, i: (0, 0)
    widths = (HG_W, 2 * HG_W, HG_W, HG_W, GQA_W, GQA_KV_W, GQA_KV_W, NA_W, NA_W, NA_W)
    dtypes = (F32, F32, F32, F32, BF16, BF16, BF16, BF16, BF16, BF16)
    return pl.pallas_call(
        functools.partial(_inproj_kernel, layer),
        grid=(b, t // tm),
        in_specs=[
            pl.BlockSpec((1, tm, d), row),
            pl.BlockSpec((1, 1, d), lambda bi, i: (bi, 0, 0)),
            pl.BlockSpec((1, 1, d), lambda bi, i: (bi, 0, 0)),
            pl.BlockSpec((d, IN_EXT), const2),
            pl.BlockSpec((DEPTH, 2, HG_W), lambda bi, i: (0, 0, 0)),
            pl.BlockSpec((tm, LANE), lambda bi, i: (i, 0)),
            pl.BlockSpec((tm, LANE), lambda bi, i: (i, 0)),
            pl.BlockSpec((1, GQA_W), const2),
            pl.BlockSpec((1, GQA_W), const2),
            pl.BlockSpec((1, GQA_KV_W), const2),
            pl.BlockSpec((1, GQA_KV_W), const2),
            pl.BlockSpec((LANE, LANE), const2),
        ],
        out_specs=[pl.BlockSpec((1, tm, w), row) for w in widths],
        out_shape=[jax.ShapeDtypeStruct((b, t, w), dt) for w, dt in zip(widths, dtypes)],
        compiler_params=_cparams(("arbitrary", "arbitrary")),
        name="inproj",
    )(x, shift, scale, w_ext, lb_logits, cos, sin, gq, gqp, gk, gkp, bd128)


def _hgrn_kernel(reverse, tb, q_ref, lf_ref, v_ref, s0_ref, tri_ref, ind_ref, bm_ref, o_ref, sfin_ref, st_scr):
    i = pl.program_id(1)
    nsc = tb // SUB

    @pl.when(i == 0)
    def _():
        st_scr[...] = s0_ref[0]

    q = q_ref[0]
    lf = lf_ref[0]
    v = v_ref[0]
    cum = jnp.dot(tri_ref[...], lf, preferred_element_type=F32, precision=lax.Precision.HIGHEST)
    kk = 1.0 - jnp.exp(lf)
    ind = ind_ref[...]
    r16 = lax.broadcasted_iota(jnp.int32, (tb, HG_W), 0) % SUB

    o = _dot((q * kk).astype(BF16), ind) * v
    for d in range(1, SUB):
        sh = (tb - d) if reverse else d
        cs = pltpu.roll(cum, sh, 0)
        ks = pltpu.roll(kk, sh, 0)
        vs = pltpu.roll(v, sh, 0)
        valid = (r16 <= SUB - 1 - d) if reverse else (r16 >= d)
        e = jnp.where(valid, q * ks * jnp.exp(cum - cs), 0.0)
        o = o + _dot(e.astype(BF16), ind) * vs
    o_ref[0] = o

    edge = 0 if reverse else SUB - 1
    cum3 = cum.reshape(nsc, SUB, HG_W)
    tot = cum3[:, edge:edge + 1, :]
    qs = (q * jnp.exp(cum)).astype(BF16)
    ks_all = (kk.reshape(nsc, SUB, HG_W) * jnp.exp(tot - cum3)).reshape(tb, HG_W).astype(BF16)
    dec = jnp.exp(tot)
    vt = v.T.astype(BF16)
    rows = lax.broadcasted_iota(jnp.int32, (tb, HG_W), 0) // SUB
    bm = bm_ref[...]
    order = range(nsc - 1, -1, -1) if reverse else range(nsc)
    for j in order:
        st = st_scr[...]
        sl = slice(j * SUB, (j + 1) * SUB)
        o_ref[0, sl, :] += _dot_nt(qs[sl], st.astype(BF16))
        ksm = jnp.where(rows == j, ks_all, jnp.zeros_like(ks_all))
        st_scr[...] = st * dec[j] + _dot(vt, ksm) * bm

    @pl.when(i == pl.num_programs(1) - 1)
    def _():
        sfin_ref[0] = st_scr[...]


def _hgrn(q, lf2, v, s0, direction, consts, tb):
    b, t, _ = q.shape
    nblk = t // tb
    reverse = direction == 1
    tri, ind, bm = consts
    if reverse:
        row = lambda bi, i: (bi, nblk - 1 - i, 0)
        rowd = lambda bi, i: (bi, nblk - 1 - i, 1)
    else:
        row = lambda bi, i: (bi, i, 0)
        rowd = lambda bi, i: (bi, i, 0)
    const2 = lambda bi, i: (0, 0)
    return pl.pallas_call(
        functools.partial(_hgrn_kernel, reverse, tb),
        grid=(b, nblk),
        in_specs=[
            pl.BlockSpec((1, tb, HG_W), row),
            pl.BlockSpec((1, tb, HG_W), rowd),
            pl.BlockSpec((1, tb, HG_W), row),
            pl.BlockSpec((1, HG_W, HG_W), lambda bi, i: (bi, 0, 0)),
            pl.BlockSpec((tb, tb), const2),
            pl.BlockSpec((HG_W, HG_W), const2),
            pl.BlockSpec((HG_W, HG_W), const2),
        ],
        out_specs=[
            pl.BlockSpec((1, tb, HG_W), row),
            pl.BlockSpec((1, HG_W, HG_W), lambda bi, i: (bi, 0, 0)),
        ],
        out_shape=[
            jax.ShapeDtypeStruct((b, t, HG_W), F32),
            jax.ShapeDtypeStruct((b, HG_W, HG_W), F32),
        ],
        scratch_shapes=[pltpu.VMEM((HG_W, HG_W), F32)],
        compiler_params=_cparams(("arbitrary", "arbitrary")),
        name="hgrn_bwd" if reverse else "hgrn_fwd",
    )(q, lf2, v, s0, tri, ind, bm)


def _hgrn_consts(tb):
    r = np.arange(tb)
    same = (r[:, None] // SUB) == (r[None, :] // SUB)
    tri_f = (same & (r[None, :] <= r[:, None])).astype(np.float32)
    tri_b = (same & (r[None, :] >= r[:, None])).astype(np.float32)
    c = np.arange(HG_W)
    blk = (c[:, None] // HEAD_DIM) == (c[None, :] // HEAD_DIM)
    ind = jnp.asarray(blk.astype(np.float32), BF16)
    bm = jnp.asarray(blk.astype(np.float32))
    return (jnp.asarray(tri_f), ind, bm), (jnp.asarray(tri_b), ind, bm)


def _half_masks(rows):
    lane = lax.broadcasted_iota(jnp.int32, (rows, LANE), 1)
    return lane < HEAD_DIM, lane >= HEAD_DIM


def _sel(mask, a):
    return jnp.where(mask, a, jnp.zeros_like(a))


def _gqa_kernel(tq, tk, n_lat, n_ctx, q_ref, kl_ref, vl_ref, kc_ref, vc_ref, o_ref, m_scr, l_scr, acc_scr):
    nslab = GQA_W // LANE
    qlo, qhi = _half_masks(tq)
    qh = []
    for s in range(nslab):
        qs = q_ref[0, :, s * LANE:(s + 1) * LANE]
        qh.append((_sel(qlo, qs), _sel(qhi, qs)))
    m_scr[...] = jnp.full(m_scr.shape, -jnp.inf, F32)
    l_scr[...] = jnp.zeros(l_scr.shape, F32)
    acc_scr[...] = jnp.zeros(acc_scr.shape, F32)

    def step(k, v):
        klo, khi = _half_masks(k.shape[0])
        vh = (_sel(klo, v), _sel(khi, v))
        for s in range(nslab):
            for half in range(2):
                idx = 2 * s + half
                sc = _dot_nt(qh[s][half], k)
                m_old = m_scr[idx]
                m_new = jnp.maximum(m_old, jnp.max(sc, axis=1, keepdims=True))
                alpha = jnp.exp(m_old - m_new)
                p = jnp.exp(sc - m_new)
                l_scr[idx] = alpha * l_scr[idx] + jnp.sum(p, axis=1, keepdims=True)
                acc_scr[idx] = alpha * acc_scr[idx] + _dot(p.astype(BF16), vh[half])
                m_scr[idx] = m_new

    def body(c, carry):
        start = pl.multiple_of(c * tk, tk)
        step(kl_ref[0, pl.ds(start, tk), :], vl_ref[0, pl.ds(start, tk), :])
        return carry

    if n_lat:
        lax.fori_loop(0, n_lat // tk, body, 0)
    step(kc_ref[0], vc_ref[0])
    for s in range(nslab):
        out = acc_scr[2 * s] / l_scr[2 * s] + acc_scr[2 * s + 1] / l_scr[2 * s + 1]
        o_ref[0, :, s * LANE:(s + 1) * LANE] = out.astype(BF16)


def _gqa(q, k_lat, v_lat, k_ctx, v_ctx, tq, tk):
    b, t, _ = q.shape
    n_lat = k_lat.shape[1]
    n_ctx = k_ctx.shape[1]
    full = lambda bi, i: (bi, 0, 0)
    return pl.pallas_call(
        functools.partial(_gqa_kernel, tq, tk, n_lat, n_ctx),
        grid=(b, t // tq),
        in_specs=[
            pl.BlockSpec((1, tq, GQA_W), lambda bi, i: (bi, i, 0)),
            pl.BlockSpec((1, n_lat, GQA_KV_W), full),
            pl.BlockSpec((1, n_lat, GQA_KV_W), full),
            pl.BlockSpec((1, n_ctx, GQA_KV_W), full),
            pl.BlockSpec((1, n_ctx, GQA_KV_W), full),
        ],
        out_specs=pl.BlockSpec((1, tq, GQA_W), lambda bi, i: (bi, i, 0)),
        out_shape=jax.ShapeDtypeStruct((b, t, GQA_W), BF16),
        scratch_shapes=[
            pltpu.VMEM((GQA_HEADS, tq, 1), F32),
            pltpu.VMEM((GQA_HEADS, tq, 1), F32),
            pltpu.VMEM((GQA_HEADS, tq, LANE), F32),
        ],
        compiler_params=_cparams(("arbitrary", "arbitrary")),
        name="gqa",
    )(q, k_lat, v_lat, k_ctx, v_ctx)


def _ctxattn_kernel(nslab, shared_kv, q_ref, k_ref, v_ref, o_ref):
    n = q_ref.shape[1]
    lo, hi = _half_masks(n)
    for s in range(nslab):
        ks = 0 if shared_kv else s
        q = q_ref[0, :, s * LANE:(s + 1) * LANE]
        k = k_ref[0, :, ks * LANE:(ks + 1) * LANE]
        v = v_ref[0, :, ks * LANE:(ks + 1) * LANE]
        out = None
        for mask in (lo, hi):
            sc = _dot_nt(_sel(mask, q), k)
            p = jnp.exp(sc - jnp.max(sc, axis=1, keepdims=True))
            o_h = _dot(p.astype(BF16), _sel(mask, v)) / jnp.sum(p, axis=1, keepdims=True)
            out = o_h if out is None else out + o_h
        o_ref[0, :, s * LANE:(s + 1) * LANE] = out.astype(BF16)


def _ctxattn(q, k, v, shared_kv):
    b, n, w = q.shape
    kw = k.shape[2]
    full = lambda bi: (bi, 0, 0)
    return pl.pallas_call(
        functools.partial(_ctxattn_kernel, w // LANE, shared_kv),
        grid=(b,),
        in_specs=[pl.BlockSpec((1, n, w), full), pl.BlockSpec((1, n, kw), full), pl.BlockSpec((1, n, kw), full)],
        out_specs=pl.BlockSpec((1, n, w), full),
        out_shape=jax.ShapeDtypeStruct((b, n, w), BF16),
        compiler_params=_cparams(("arbitrary",)),
        name="ctxattn",
    )(q, k, v)


def _na_window_start(i, r, rows):
    return jnp.clip(i * r - NA_KH // 2, 0, rows - (r + NA_KH))


def _na_kernel(r, rows, q_ref, k_ref, v_ref, kc_ref, vc_ref, eb_ref, o_ref):
    i = pl.program_id(1)
    tq = r * GRID_W
    kwin = (r + NA_KH) * GRID_W
    start = pl.multiple_of(_na_window_start(i, r, rows) * GRID_W, GRID_W)
    qlo, qhi = _half_masks(tq)
    wlo, whi = _half_masks(kwin)
    clo, chi = _half_masks(kc_ref.shape[1])
    for s in range(NA_W // LANE):
        sl = slice(s * LANE, (s + 1) * LANE)
        q = q_ref[0, :, sl]
        kw = k_ref[0, pl.ds(start, kwin), sl]
        vw = v_ref[0, pl.ds(start, kwin), sl]
        kc = kc_ref[0, :, sl]
        vc = vc_ref[0, :, sl]
        out = None
        for half, (qm, wm, cm) in enumerate(((qlo, wlo, clo), (qhi, whi, chi))):
            qh = _sel(qm, q)
            s_loc = _dot_nt(qh, kw) + eb_ref[0, 2 * s + half]
            s_ctx = _dot_nt(qh, kc)
            m = jnp.maximum(jnp.max(s_loc, axis=1, keepdims=True), jnp.max(s_ctx, axis=1, keepdims=True))
            p_loc = jnp.exp(s_loc - m)
            p_ctx = jnp.exp(s_ctx - m)
            den = jnp.sum(p_loc, axis=1, keepdims=True) + jnp.sum(p_ctx, axis=1, keepdims=True)
            o_h = (_dot(p_loc.astype(BF16), _sel(wm, vw)) + _dot(p_ctx.astype(BF16), _sel(cm, vc))) / den
            out = o_h if out is None else out + o_h
        o_ref[0, :, sl] = out.astype(BF16)


def _na(q, k, v, k_ctx, v_ctx, eb, r):
    b, t, w = q.shape
    rows = t // GRID_W
    nblk = rows // r
    tq = r * GRID_W
    kwin = (r + NA_KH) * GRID_W
    n_ctx = k_ctx.shape[1]
    full = lambda bi, i: (bi, 0, 0)

    def variant(bi, i):
        return (jnp.where(i == 0, 0, jnp.where(i == nblk - 1, 2, 1)), 0, 0, 0)

    return pl.pallas_call(
        functools.partial(_na_kernel, r, rows),
        grid=(b, nblk),
        in_specs=[
            pl.BlockSpec((1, tq, w), lambda bi, i: (bi, i, 0)),
            pl.BlockSpec((1, t, w), full),
            pl.BlockSpec((1, t, w), full),
            pl.BlockSpec((1, n_ctx, w), full),
            pl.BlockSpec((1, n_ctx, w), full),
            pl.BlockSpec((1, NA_HEADS, tq, kwin), variant),
        ],
        out_specs=pl.BlockSpec((1, tq, w), lambda bi, i: (bi, i, 0)),
        out_shape=jax.ShapeDtypeStruct((b, t, w), BF16),
        compiler_params=_cparams(("arbitrary", "arbitrary")),
        name="na",
    )(q, k, v, k_ctx, v_ctx, eb)


def _na_bias_tables(rpb, r, rows):
    nblk = rows // r
    ql = np.arange(r * GRID_W)
    kl = np.arange((r + NA_KH) * GRID_W)
    qa, qc = ql // GRID_W, ql % GRID_W
    kb, kc = kl // GRID_W, kl % GRID_W
    cs = np.clip(qc - NA_KW // 2, 0, GRID_W - NA_KW)
    col_ok = (kc[None, :] >= cs[:, None]) & (kc[None, :] < cs[:, None] + NA_KW)
    dc = np.clip(kc[None, :] - qc[:, None] + NA_KW - 1, 0, 2 * NA_KW - 2)
    tabs = []
    for blk in (0, min(1, nblk - 1), nblk - 1):
        r0 = blk * r
        lo = int(np.clip(r0 - NA_KH // 2, 0, rows - (r + NA_KH)))
        qr = r0 + qa
        kr = lo + kb
        rs = np.clip(qr - NA_KH // 2, 0, rows - NA_KH)
        row_ok = (kr[None, :] >= rs[:, None]) & (kr[None, :] < rs[:, None] + NA_KH)
        dr = np.clip(kr[None, :] - qr[:, None] + NA_KH - 1, 0, 2 * NA_KH - 2)
        bias = rpb[:, dr, dc]
        tabs.append(jnp.where(jnp.asarray(row_ok & col_ok)[None], bias, NEG))
    return jnp.stack(tabs)


def _layer_norm(x, g, b):
    xc = x - jnp.mean(x, axis=-1, keepdims=True)
    return xc * lax.rsqrt(jnp.mean(xc * xc, axis=-1, keepdims=True) + NORM_EPS) * g + b


def _post_kernel(hc, x_ref, of_ref, ob_ref, sg_ref, gq_ref, na_ref, mod_ref, hn_ref, bd_ref, wo_ref, w1_ref, w2_ref,
                 ln_ref, o_ref):
    x = x_ref[0]
    g1, sh2, sc2, g2 = (mod_ref[0, k:k + 1, :] for k in range(4))
    o = of_ref[0] + ob_ref[0]
    ms = _dot(o * o, bd_ref[...])
    hg = (o * lax.rsqrt(ms + NORM_EPS) * hn_ref[...] * sg_ref[0]).astype(BF16)
    y = (_dot(hg, wo_ref[0:HG_W, :]) + _dot(gq_ref[0], wo_ref[HG_W:HG_W + GQA_W, :])
         + _dot(na_ref[0], wo_ref[HG_W + GQA_W:, :]))
    x1 = _layer_norm(DN_ALPHA * x + g1 * y, ln_ref[0:1, :], ln_ref[1:2, :])
    h2 = (x1 * (1.0 + sc2) + sh2).astype(BF16)
    f = None
    for c in range(FFN_HIDDEN // hc):
        ug = _dot(h2, w1_ref[:, c * hc:(c + 1) * hc])
        uu = _dot(h2, w1_ref[:, FFN_HIDDEN + c * hc:FFN_HIDDEN + (c + 1) * hc])
        a = (ug * _sigmoid(ug) * uu).astype(BF16)
        part = _dot(a, w2_ref[c * hc:(c + 1) * hc, :])
        f = part if f is None else f + part
    o_ref[0] = _layer_norm(DN_ALPHA * x1 + g2 * f, ln_ref[2:3, :], ln_ref[3:4, :])


def _post(x, o_f, o_b, sg, gq, na, mod4, hn, bd256, w_out, w1, w2, ln4, tm, hc):
    b, t, d = x.shape
    row = lambda bi, i: (bi, i, 0)
    const2 = lambda bi, i: (0, 0)
    return pl.pallas_call(
        functools.partial(_post_kernel, hc),
        grid=(b, t // tm),
        in_specs=[
            pl.BlockSpec((1, tm, d), row),
            pl.BlockSpec((1, tm, HG_W), row),
            pl.BlockSpec((1, tm, HG_W), row),
            pl.BlockSpec((1, tm, HG_W), row),
            pl.BlockSpec((1, tm, GQA_W), row),
            pl.BlockSpec((1, tm, NA_W), row),
            pl.BlockSpec((1, 4, d), lambda bi, i: (bi, 0, 0)),
            pl.BlockSpec((1, HG_W), const2),
            pl.BlockSpec((HG_W, HG_W), const2),
            pl.BlockSpec((d, d), const2),
            pl.BlockSpec((d, 2 * FFN_HIDDEN), const2),
            pl.BlockSpec((FFN_HIDDEN, d), const2),
            pl.BlockSpec((4, d), const2),
        ],
        out_specs=pl.BlockSpec((1, tm, d), row),
        out_shape=jax.ShapeDtypeStruct((b, t, d), F32),
        compiler_params=_cparams(("arbitrary", "arbitrary")),
        name="post",
    )(x, o_f, o_b, sg, gq, na, mod4, hn, bd256, w_out, w1, w2, ln4)


_PERM_MAIN = np.concatenate([np.arange(0, HEAD_DIM, 2), np.arange(1, HEAD_DIM, 2)])
_PERM_PART = np.concatenate([np.arange(1, HEAD_DIM, 2), np.arange(0, HEAD_DIM, 2)])


def _in_ext_columns():
    base = np.arange(5 * HG_W)
    gq0 = 5 * HG_W
    gk0 = gq0 + GQA_W
    gv0 = gk0 + GQA_KV_W
    nq0 = gv0 + GQA_KV_W
    gq_main = np.concatenate([gq0 + h * HEAD_DIM + _PERM_MAIN for h in GQA_SLAB_HEADS])
    gq_part = np.concatenate([gq0 + h * HEAD_DIM + _PERM_PART for h in GQA_SLAB_HEADS])
    gk_main = np.concatenate([gk0 + h * HEAD_DIM + _PERM_MAIN for h in range(GQA_KV_HEADS)])
    gk_part = np.concatenate([gk0 + h * HEAD_DIM + _PERM_PART for h in range(GQA_KV_HEADS)])
    gv = gv0 + np.arange(GQA_KV_W)
    na = nq0 + np.arange(3 * NA_W)
    return np.concatenate([base, gq_main, gk_main, gv, na, gq_part, gk_part])


def _rope_tables(n_tokens):
    t = np.arange(n_tokens)
    row = (t // GRID_W).astype(np.float32)
    col = (t % GRID_W).astype(np.float32)
    n_pairs = HEAD_DIM // 4
    inv_freq = np.exp(-math.log(ROPE_THETA) * np.arange(n_pairs, dtype=np.float32) / n_pairs).astype(np.float32)
    ang = np.concatenate([row[:, None] * inv_freq, col[:, None] * inv_freq], -1).astype(np.float32)
    ang = jnp.asarray(ang)
    c, s = jnp.cos(ang), jnp.sin(ang)
    cos = jnp.tile(jnp.concatenate([c, c], -1), (1, LANE // HEAD_DIM))
    sin = jnp.tile(jnp.concatenate([-s, s], -1), (1, LANE // HEAD_DIM))
    return cos, sin


def _block_diag_mean(width):
    c = np.arange(width)
    return jnp.asarray(((c[:, None] // HEAD_DIM) == (c[None, :] // HEAD_DIM)).astype(np.float32) / HEAD_DIM)


def kernel(x, c, ctx, c_ctx, w_ada, b_ada, w_in, hgrn_lb_logits, hgrn_norm, gqa_q_norm, gqa_k_norm, na_rpb, w_out,
           w_ffn_in, w_ffn_out, ln_g, ln_b):
    b, n, d = x.shape
    n_ctx = ctx.shape[1]
    rows = n // GRID_W
    tm = min(512, n)
    tm_ctx = min(256, n_ctx)
    tb = 256
    tq, tk = min(256, n), min(512, n)
    na_r = 4
    hc = FFN_HIDDEN // 2

    cv = jnp.zeros((MOD_ROWS, d), F32).at[:b].set(c).at[b].set(c_ctx)
    mods = _mods(cv, w_ada, b_ada)

    cols = _in_ext_columns()
    w_ext = w_in[:, :, cols].astype(BF16)
    out_rows = np.concatenate(
        [np.arange(HG_W)] + [HG_W + h * HEAD_DIM + np.arange(HEAD_DIM) for h in GQA_SLAB_HEADS]
        + [HG_W + GQA_W + np.arange(NA_W)])
    w_out_p = w_out[:, out_rows, :].astype(BF16)
    w1 = w_ffn_in.astype(BF16)
    w2 = w_ffn_out.astype(BF16)

    cos_l, sin_l = _rope_tables(n)
    cos_c = jnp.ones((n_ctx, LANE), F32)
    sin_c = jnp.zeros((n_ctx, LANE), F32)
    bd128 = _block_diag_mean(LANE)
    bd256 = _block_diag_mean(HG_W)
    hg_consts = _hgrn_consts(tb)
    s_zero = jnp.zeros((b, HG_W, HG_W), F32)

    x_lat, x_ctx = x, ctx
    for l in range(DEPTH):
        need_ctx = l < DEPTH - 1
        m = mods[l].reshape(MOD_ROWS, N_MOD, d)
        m_lat = m[:b]
        m_ctx = jnp.broadcast_to(m[b][None], (b, N_MOD, d))
        qscale = HEAD_DIM ** -0.5
        gains = (
            (jnp.tile(gqa_q_norm[l][_PERM_MAIN], GQA_HEADS) * qscale)[None],
            (jnp.tile(gqa_q_norm[l][_PERM_PART], GQA_HEADS) * qscale)[None],
            jnp.tile(gqa_k_norm[l][_PERM_MAIN], GQA_KV_HEADS)[None],
            jnp.tile(gqa_k_norm[l][_PERM_PART], GQA_KV_HEADS)[None],
        )
        p_lat = _inproj(l, x_lat, m_lat[:, 0:1], m_lat[:, 1:2], w_ext[l], hgrn_lb_logits, cos_l, sin_l, gains, bd128, tm)
        p_ctx = _inproj(l, x_ctx, m_ctx[:, 0:1], m_ctx[:, 1:2], w_ext[l], hgrn_lb_logits, cos_c, sin_c, gains, bd128,
                        tm_ctx)
        hq_l, lf_l, hi_l, sg_l, gq_l, gk_l, gv_l, nq_l, nk_l, nv_l = p_lat
        hq_c, lf_c, hi_c, sg_c, gq_c, gk_c, gv_c, nq_c, nk_c, nv_c = p_ctx

        o_c, o_l = [], []
        for direction in range(2):
            oc, s_c = _hgrn(hq_c, lf_c, hi_c, s_zero, direction, hg_consts[direction], min(tb, n_ctx))
            ol, _ = _hgrn(hq_l, lf_l, hi_l, s_c, direction, hg_consts[direction], tb)
            o_c.append(oc)
            o_l.append(ol)

        gqa_l = _gqa(gq_l, gk_l, gv_l, gk_c, gv_c, tq, tk)
        eb = _na_bias_tables(na_rpb[l], na_r, rows)
        na_l = _na(nq_l, nk_l, nv_l, nk_c, nv_c, eb, na_r)

        hn = jnp.tile(hgrn_norm[l], HG_HEADS)[None]
        ln4 = jnp.stack([ln_g[l, 0], ln_b[l, 0], ln_g[l, 1], ln_b[l, 1]])
        if need_ctx:
            gqa_c = _ctxattn(gq_c, gk_c, gv_c, True)
            na_c = _ctxattn(nq_c, nk_c, nv_c, False)
            x_ctx = _post(x_ctx, o_c[0], o_c[1], sg_c, gqa_c, na_c, m_ctx[:, 2:6], hn, bd256, w_out_p[l], w1[l], w2[l],
                          ln4, tm_ctx, hc)
        x_lat = _post(x_lat, o_l[0], o_l[1], sg_l, gqa_l, na_l, m_lat[:, 2:6], hn, bd256, w_out_p[l], w1[l], w2[l],
                      ln4, tm, hc)
    return x_lat
```

```python
import functools
import math

import numpy as np
import jax
import jax.numpy as jnp
from jax import lax
from jax.experimental import pallas as pl
from jax.experimental.pallas import tpu as pltpu

F32 = jnp.float32
BF16 = jnp.bfloat16

D_MODEL = 1024
DEPTH = 2
GRID_W = 64
HEAD_DIM = 64
HG_HEADS = 4
GQA_HEADS = 6
GQA_KV_HEADS = 2
NA_HEADS = 6
NA_KH = 8
NA_KW = 16
ROPE_THETA = 10000.0
NORM_EPS = 1e-6
N_MOD = 6
HG_W = HG_HEADS * HEAD_DIM
GQA_W = GQA_HEADS * HEAD_DIM
GQA_KV_W = GQA_KV_HEADS * HEAD_DIM
NA_W = NA_HEADS * HEAD_DIM
FFN_HIDDEN = 2816
DN_ALPHA = (2 * DEPTH) ** 0.25

LANE = 128
SUB = 16
MOD_ROWS = 8
NEG = -1e30
VMEM_LIMIT = 56 * 1024 * 1024

C_HQ, C_FF, C_FB, C_HI, C_HGATE = 0, 256, 512, 768, 1024
C_GQ, C_GK, C_GV = 1280, 1664, 1792
C_NQ, C_NK, C_NV = 1920, 2304, 2688
C_GQP, C_GKP = 3072, 3456
IN_EXT = 3584
GQA_SLAB_HEADS = (0, 3, 1, 4, 2, 5)


def _cparams(sem, flags=None):
    return pltpu.CompilerParams(dimension_semantics=sem, vmem_limit_bytes=VMEM_LIMIT, flags=flags)


def _sigmoid(z):
    return 1.0 / (1.0 + jnp.exp(-z))


def _dot(a, b):
    return jnp.dot(a, b, preferred_element_type=F32)


def _dot_nt(a, b):
    return lax.dot_general(a, b, (((1,), (1,)), ((), ())), preferred_element_type=F32)


def _mods_kernel(cv_ref, w_ref, b_ref, o_ref):
    cv = cv_ref[...]
    a = cv * _sigmoid(cv)
    o_ref[0] = jnp.dot(a, w_ref[0], preferred_element_type=F32, precision=lax.Precision.HIGHEST) + b_ref[0]


def _mods(cv, w_ada, b_ada):
    depth, d, nm = w_ada.shape
    tn = 1024
    return pl.pallas_call(
        _mods_kernel,
        grid=(depth, nm // tn),
        in_specs=[
            pl.BlockSpec((MOD_ROWS, d), lambda l, j: (0, 0)),
            pl.BlockSpec((1, d, tn), lambda l, j: (l, 0, j)),
            pl.BlockSpec((1, 1, tn), lambda l, j: (l, 0, j)),
        ],
        out_specs=pl.BlockSpec((1, MOD_ROWS, tn), lambda l, j: (l, 0, j)),
        out_shape=jax.ShapeDtypeStruct((depth, MOD_ROWS, nm), F32),
        compiler_params=_cparams(("arbitrary", "arbitrary")),
        name="mods",
    )(cv, w_ada, b_ada.reshape(depth, 1, nm))


def _inproj_kernel(layer, x_ref, sh_ref, sc_ref, w_ref, lbl_ref, cos_ref, sin_ref, gq_ref, gqp_ref, gk_ref, gkp_ref,
                   bd_ref, hq_ref, lf_ref, hi_ref, hg_ref, gq_o, gk_o, gv_o, nq_o, nk_o, nv_o, gvt_o):
    x = x_ref[0]
    h = (x * (1.0 + sc_ref[0]) + sh_ref[0]).astype(BF16)

    def proj(c0, width):
        return _dot(h, w_ref[:, c0:c0 + width])

    logits = [lbl_ref[d] for d in range(DEPTH)]
    mx = functools.reduce(jnp.maximum, logits)
    es = [jnp.exp(v - mx) for v in logits]
    inv = 1.0 / functools.reduce(lambda a, b: a + b, es)
    ps = [e * inv for e in es]
    lb = jnp.clip(functools.reduce(lambda a, b: a + b, ps[:layer + 1]) - ps[0], 0.0, 1.0)

    q = proj(C_HQ, HG_W)
    hq_ref[0] = q * _sigmoid(q)
    for d, c0 in enumerate((C_FF, C_FB)):
        z = proj(c0, HG_W)
        lbd = lb[d:d + 1, :]
        lf_ref[0, :, d * HG_W:(d + 1) * HG_W] = jnp.log(lbd + (1.0 - lbd) * _sigmoid(z))
    hi_ref[0] = proj(C_HI, HG_W)
    g = proj(C_HGATE, HG_W)
    hg_ref[0] = g * _sigmoid(g)

    cos = cos_ref[...]
    sin = sin_ref[...]
    bd = bd_ref[...]

    def normed_rope(c_main, c_part, gm_ref, gp_ref, out_ref, nslab):
        for s in range(nslab):
            pm = proj(c_main + s * LANE, LANE)
            pp = proj(c_part + s * LANE, LANE)
            ms = _dot(pm * pm, bd)
            r = lax.rsqrt(ms + NORM_EPS)
            gm = gm_ref[:, s * LANE:(s + 1) * LANE]
            gp = gp_ref[:, s * LANE:(s + 1) * LANE]
            out_ref[0, :, s * LANE:(s + 1) * LANE] = ((pm * r * gm) * cos + (pp * r * gp) * sin).astype(BF16)

    normed_rope(C_GQ, C_GQP, gq_ref, gqp_ref, gq_o, GQA_W // LANE)
    normed_rope(C_GK, C_GKP, gk_ref, gkp_ref, gk_o, GQA_KV_W // LANE)
    gv = proj(C_GV, GQA_KV_W)
    gv_o[0] = gv.astype(BF16)
    lane = lax.broadcasted_iota(jnp.int32, gv.shape, 1)
    gvt_o[0, 0] = jnp.where(lane < HEAD_DIM, gv, jnp.where(lane == HEAD_DIM, 1.0, 0.0)).T.astype(BF16)
    gvt_o[0, 1] = jnp.where(lane >= HEAD_DIM, gv, jnp.where(lane == 0, 1.0, 0.0)).T.astype(BF16)
    nq_o[0] = (proj(C_NQ, NA_W) * (HEAD_DIM ** -0.5)).astype(BF16)
    nk_o[0] = proj(C_NK, NA_W).astype(BF16)
    nv_o[0] = proj(C_NV, NA_W).astype(BF16)


def _inproj(layer, x, shift, scale, w_ext, lb_logits, cos, sin, gains, bd128, tm):
    b, t, d = x.shape
    gq, gqp, gk, gkp = gains
    row = lambda bi, i: (bi, i, 0)
    const2 = lambda bi, i: (0, 0)
    widths = (HG_W, 2 * HG_W, HG_W, HG_W, GQA_W, GQA_KV_W, GQA_KV_W, NA_W, NA_W, NA_W)
    dtypes = (F32, F32, F32, F32, BF16, BF16, BF16, BF16, BF16, BF16)
    return pl.pallas_call(
        functools.partial(_inproj_kernel, layer),
        grid=(b, t // tm),
        in_specs=[
            pl.BlockSpec((1, tm, d), row),
            pl.BlockSpec((1, 1, d), lambda bi, i: (bi, 0, 0)),
            pl.BlockSpec((1, 1, d), lambda bi, i: (bi, 0, 0)),
            pl.BlockSpec((d, IN_EXT), const2),
            pl.BlockSpec((DEPTH, 2, HG_W), lambda bi, i: (0, 0, 0)),
            pl.BlockSpec((tm, LANE), lambda bi, i: (i, 0)),
            pl.BlockSpec((tm, LANE), lambda bi, i: (i, 0)),
            pl.BlockSpec((1, GQA_W), const2),
            pl.BlockSpec((1, GQA_W), const2),
            pl.BlockSpec((1, GQA_KV_W), const2),
            pl.BlockSpec((1, GQA_KV_W), const2),
            pl.BlockSpec((LANE, LANE), const2),
        ],
        out_specs=[pl.BlockSpec((1, tm, w), row) for w in widths]
        + [pl.BlockSpec((1, 2, LANE, tm), lambda bi, i: (bi, 0, 0, i))],
        out_shape=[jax.ShapeDtypeStruct((b, t, w), dt) for w, dt in zip(widths, dtypes)]
        + [jax.ShapeDtypeStruct((b, 2, LANE, t), BF16)],
        compiler_params=_cparams(("arbitrary", "arbitrary")),
        name="inproj",
    )(x, shift, scale, w_ext, lb_logits, cos, sin, gq, gqp, gk, gkp, bd128)


def _hgrn_kernel(reverse, tb, q_ref, lf_ref, v_ref, s0_ref, tri_ref, ind_ref, bm_ref, o_ref, sfin_ref, st_scr):
    i = pl.program_id(1)
    nsc = tb // SUB

    @pl.when(i == 0)
    def _():
        st_scr[...] = s0_ref[0]

    q = q_ref[0]
    lf = lf_ref[0]
    v = v_ref[0]
    cum = jnp.dot(tri_ref[...], lf, preferred_element_type=F32, precision=lax.Precision.HIGHEST)
    kk = 1.0 - jnp.exp(lf)
    ind = ind_ref[...]
    r16 = lax.broadcasted_iota(jnp.int32, (tb, HG_W), 0) % SUB

    o = _dot((q * kk).astype(BF16), ind) * v
    for d in range(1, SUB):
        sh = (tb - d) if reverse else d
        cs = pltpu.roll(cum, sh, 0)
        ks = pltpu.roll(kk, sh, 0)
        vs = pltpu.roll(v, sh, 0)
        valid = (r16 <= SUB - 1 - d) if reverse else (r16 >= d)
        e = jnp.where(valid, q * ks * jnp.exp(cum - cs), 0.0)
        o = o + _dot(e.astype(BF16), ind) * vs
    o_ref[0] = o

    edge = 0 if reverse else SUB - 1
    cum3 = cum.reshape(nsc, SUB, HG_W)
    tot = cum3[:, edge:edge + 1, :]
    qs = (q * jnp.exp(cum)).astype(BF16)
    ks_all = (kk.reshape(nsc, SUB, HG_W) * jnp.exp(tot - cum3)).reshape(tb, HG_W).astype(BF16)
    dec = jnp.exp(tot)
    vt = v.T.astype(BF16)
    rows = lax.broadcasted_iota(jnp.int32, (tb, HG_W), 0) // SUB
    bm = bm_ref[...]
    order = range(nsc - 1, -1, -1) if reverse else range(nsc)
    for j in order:
        st = st_scr[...]
        sl = slice(j * SUB, (j + 1) * SUB)
        o_ref[0, sl, :] += _dot_nt(qs[sl], st.astype(BF16))
        ksm = jnp.where(rows == j, ks_all, jnp.zeros_like(ks_all))
        st_scr[...] = st * dec[j] + _dot(vt, ksm) * bm

    @pl.when(i == pl.num_programs(1) - 1)
    def _():
        sfin_ref[0] = st_scr[...]


def _hgrn(q, lf2, v, s0, direction, consts, tb):
    b, t, _ = q.shape
    nblk = t // tb
    reverse = direction == 1
    tri, ind, bm = consts
    if reverse:
        row = lambda bi, i: (bi, nblk - 1 - i, 0)
        rowd = lambda bi, i: (bi, nblk - 1 - i, 1)
    else:
        row = lambda bi, i: (bi, i, 0)
        rowd = lambda bi, i: (bi, i, 0)
    const2 = lambda bi, i: (0, 0)
    return pl.pallas_call(
        functools.partial(_hgrn_kernel, reverse, tb),
        grid=(b, nblk),
        in_specs=[
            pl.BlockSpec((1, tb, HG_W), row),
            pl.BlockSpec((1, tb, HG_W), rowd),
            pl.BlockSpec((1, tb, HG_W), row),
            pl.BlockSpec((1, HG_W, HG_W), lambda bi, i: (bi, 0, 0)),
            pl.BlockSpec((tb, tb), const2),
            pl.BlockSpec((HG_W, HG_W), const2),
            pl.BlockSpec((HG_W, HG_W), const2),
        ],
        out_specs=[
            pl.BlockSpec((1, tb, HG_W), row),
            pl.BlockSpec((1, HG_W, HG_W), lambda bi, i: (bi, 0, 0)),
        ],
        out_shape=[
            jax.ShapeDtypeStruct((b, t, HG_W), F32),
            jax.ShapeDtypeStruct((b, HG_W, HG_W), F32),
        ],
        scratch_shapes=[pltpu.VMEM((HG_W, HG_W), F32)],
        compiler_params=_cparams(("arbitrary", "arbitrary")),
        name="hgrn_bwd" if reverse else "hgrn_fwd",
    )(q, lf2, v, s0, tri, ind, bm)


def _hgrn_consts(tb):
    r = np.arange(tb)
    same = (r[:, None] // SUB) == (r[None, :] // SUB)
    tri_f = (same & (r[None, :] <= r[:, None])).astype(np.float32)
    tri_b = (same & (r[None, :] >= r[:, None])).astype(np.float32)
    c = np.arange(HG_W)
    blk = (c[:, None] // HEAD_DIM) == (c[None, :] // HEAD_DIM)
    ind = jnp.asarray(blk.astype(np.float32), BF16)
    bm = jnp.asarray(blk.astype(np.float32))
    return (jnp.asarray(tri_f), ind, bm), (jnp.asarray(tri_b), ind, bm)


def _half_masks(rows):
    lane = lax.broadcasted_iota(jnp.int32, (rows, LANE), 1)
    return lane < HEAD_DIM, lane >= HEAD_DIM


def _sel(mask, a):
    return jnp.where(mask, a, jnp.zeros_like(a))


def _value_slabs(v):
    lane = lax.broadcasted_iota(jnp.int32, v.shape, 1)
    lo, hi = _half_masks(v.shape[0])
    e_lo = jnp.where(lane == HEAD_DIM, 1.0, 0.0).astype(v.dtype)
    e_hi = jnp.where(lane == 0, 1.0, 0.0).astype(v.dtype)
    return _sel(lo, v) + e_lo, _sel(hi, v) + e_hi


def _normalize_slab(a_lo, a_hi):
    lane = lax.broadcasted_iota(jnp.int32, a_lo.shape, 1)
    return jnp.where(lane < HEAD_DIM, a_lo / a_lo[:, HEAD_DIM:HEAD_DIM + 1], a_hi / a_hi[:, 0:1])


def _gqa_kernel(tq, tk, n_lat, q_ref, kl_ref, vtl_ref, kc_ref, vtc_ref, o_ref, m_scr, acc_scr):
    nslab = GQA_W // LANE
    qlo, qhi = _half_masks(tq)
    qh = []
    for s in range(nslab):
        qs = q_ref[0, :, s * LANE:(s + 1) * LANE]
        qh.append((_sel(qlo, qs), _sel(qhi, qs)))
    m_scr[...] = jnp.full(m_scr.shape, -jnp.inf, F32)
    acc_scr[...] = jnp.zeros(acc_scr.shape, F32)

    heads = [(s, half) for s in range(nslab) for half in range(2)]

    def step(k, vt):
        ahead = 6
        sts = [_dot_nt(k, qh[s][half]) for (s, half) in heads[:ahead]]
        for idx, (s, half) in enumerate(heads):
            st = sts[idx]
            if idx + ahead < len(heads):
                s2, half2 = heads[idx + ahead]
                sts.append(_dot_nt(k, qh[s2][half2]))
            m_old = m_scr[idx]
            m_new = jnp.maximum(m_old, jnp.max(st, axis=0, keepdims=True))
            alpha = jnp.exp2(m_old - m_new)
            pt = jnp.exp2((st - m_new).astype(BF16))
            acc_scr[idx] = alpha * acc_scr[idx] + _dot(vt[half], pt)
            m_scr[idx] = m_new

    def body(c, carry):
        start = pl.multiple_of(c * tk, tk)
        step(kl_ref[0, pl.ds(start, tk), :], [vtl_ref[0, h, :, pl.ds(start, tk)] for h in range(2)])
        return carry

    lax.fori_loop(0, n_lat // tk, body, 0)
    step(kc_ref[0], [vtc_ref[0, h] for h in range(2)])
    row = lax.broadcasted_iota(jnp.int32, (LANE, tq), 0)
    for s in range(nslab):
        a_lo, a_hi = acc_scr[2 * s], acc_scr[2 * s + 1]
        out_t = jnp.where(row < HEAD_DIM, a_lo / a_lo[HEAD_DIM:HEAD_DIM + 1, :], a_hi / a_hi[0:1, :])
        o_ref[0, :, s * LANE:(s + 1) * LANE] = out_t.T.astype(BF16)


def _gqa(q, k_lat, vt_lat, k_ctx, vt_ctx, tq, tk):
    b, t, _ = q.shape
    n_lat = k_lat.shape[1]
    n_ctx = k_ctx.shape[1]
    full = lambda bi, i: (bi, 0, 0)
    full4 = lambda bi, i: (bi, 0, 0, 0)
    return pl.pallas_call(
        functools.partial(_gqa_kernel, tq, tk, n_lat),
        grid=(b, t // tq),
        in_specs=[
            pl.BlockSpec((1, tq, GQA_W), lambda bi, i: (bi, i, 0)),
            pl.BlockSpec((1, n_lat, GQA_KV_W), full),
            pl.BlockSpec((1, 2, LANE, n_lat), full4),
            pl.BlockSpec((1, n_ctx, GQA_KV_W), full),
            pl.BlockSpec((1, 2, LANE, n_ctx), full4),
        ],
        out_specs=pl.BlockSpec((1, tq, GQA_W), lambda bi, i: (bi, i, 0)),
        out_shape=jax.ShapeDtypeStruct((b, t, GQA_W), BF16),
        scratch_shapes=[
            pltpu.VMEM((GQA_HEADS, 1, tq), F32),
            pltpu.VMEM((GQA_HEADS, LANE, tq), F32),
        ],
        compiler_params=_cparams(("arbitrary", "arbitrary")),
        name="gqa",
    )(q, k_lat, vt_lat, k_ctx, vt_ctx)


def _ctxattn_kernel(nslab, shared_kv, q_ref, k_ref, v_ref, o_ref):
    n = q_ref.shape[1]
    lo, hi = _half_masks(n)
    ex = jnp.exp2 if shared_kv else jnp.exp
    for s in range(nslab):
        ks = 0 if shared_kv else s
        q = q_ref[0, :, s * LANE:(s + 1) * LANE]
        k = k_ref[0, :, ks * LANE:(ks + 1) * LANE]
        v = v_ref[0, :, ks * LANE:(ks + 1) * LANE]
        out = None
        for mask in (lo, hi):
            sc = _dot_nt(_sel(mask, q), k)
            p = ex(sc - jnp.max(sc, axis=1, keepdims=True))
            o_h = _dot(p.astype(BF16), _sel(mask, v)) / jnp.sum(p, axis=1, keepdims=True)
            out = o_h if out is None else out + o_h
        o_ref[0, :, s * LANE:(s + 1) * LANE] = out.astype(BF16)


def _ctxattn(q, k, v, shared_kv):
    b, n, w = q.shape
    kw = k.shape[2]
    full = lambda bi: (bi, 0, 0)
    return pl.pallas_call(
        functools.partial(_ctxattn_kernel, w // LANE, shared_kv),
        grid=(b,),
        in_specs=[pl.BlockSpec((1, n, w), full), pl.BlockSpec((1, n, kw), full), pl.BlockSpec((1, n, kw), full)],
        out_specs=pl.BlockSpec((1, n, w), full),
        out_shape=jax.ShapeDtypeStruct((b, n, w), BF16),
        compiler_params=_cparams(("arbitrary",)),
        name="ctxattn",
    )(q, k, v)


def _na_window_start(i, r, rows):
    return jnp.clip(i * r - NA_KH // 2, 0, rows - (r + NA_KH))


def _na_kernel(r, rows, q_ref, k_ref, v_ref, kc_ref, vc_ref, eb_ref, o_ref):
    i = pl.program_id(1)
    tq = r * GRID_W
    kwin = (r + NA_KH) * GRID_W
    start = pl.multiple_of(_na_window_start(i, r, rows) * GRID_W, GRID_W)
    qlo, qhi = _half_masks(tq)
    wlo, whi = _half_masks(kwin)
    clo, chi = _half_masks(kc_ref.shape[1])
    for s in range(NA_W // LANE):
        sl = slice(s * LANE, (s + 1) * LANE)
        q = q_ref[0, :, sl]
        kw = k_ref[0, pl.ds(start, kwin), sl]
        vw = v_ref[0, pl.ds(start, kwin), sl]
        kc = kc_ref[0, :, sl]
        vc = vc_ref[0, :, sl]
        out = None
        for half, (qm, wm, cm) in enumerate(((qlo, wlo, clo), (qhi, whi, chi))):
            qh = _sel(qm, q)
            s_loc = _dot_nt(qh, kw) + eb_ref[0, 2 * s + half]
            s_ctx = _dot_nt(qh, kc)
            m = jnp.maximum(jnp.max(s_loc, axis=1, keepdims=True), jnp.max(s_ctx, axis=1, keepdims=True))
            p_loc = jnp.exp(s_loc - m)
            p_ctx = jnp.exp(s_ctx - m)
            den = jnp.sum(p_loc, axis=1, keepdims=True) + jnp.sum(p_ctx, axis=1, keepdims=True)
            o_h = (_dot(p_loc.astype(BF16), _sel(wm, vw)) + _dot(p_ctx.astype(BF16), _sel(cm, vc))) / den
            out = o_h if out is None else out + o_h
        o_ref[0, :, sl] = out.astype(BF16)


def _na(q, k, v, k_ctx, v_ctx, eb, r):
    b, t, w = q.shape
    rows = t // GRID_W
    nblk = rows // r
    tq = r * GRID_W
    kwin = (r + NA_KH) * GRID_W
    n_ctx = k_ctx.shape[1]
    full = lambda bi, i: (bi, 0, 0)

    def variant(bi, i):
        return (jnp.where(i == 0, 0, jnp.where(i == nblk - 1, 2, 1)), 0, 0, 0)

    return pl.pallas_call(
        functools.partial(_na_kernel, r, rows),
        grid=(b, nblk),
        in_specs=[
            pl.BlockSpec((1, tq, w), lambda bi, i: (bi, i, 0)),
            pl.BlockSpec((1, t, w), full),
            pl.BlockSpec((1, t, w), full),
            pl.BlockSpec((1, n_ctx, w), full),
            pl.BlockSpec((1, n_ctx, w), full),
            pl.BlockSpec((1, NA_HEADS, tq, kwin), variant),
        ],
        out_specs=pl.BlockSpec((1, tq, w), lambda bi, i: (bi, i, 0)),
        out_shape=jax.ShapeDtypeStruct((b, t, w), BF16),
        compiler_params=_cparams(("arbitrary", "arbitrary")),
        name="na",
    )(q, k, v, k_ctx, v_ctx, eb)


def _na_bias_tables(rpb, r, rows):
    nblk = rows // r
    nkr = r + NA_KH
    qc = np.arange(GRID_W)
    kc = np.arange(GRID_W)
    cs = np.clip(qc - NA_KW // 2, 0, GRID_W - NA_KW)
    col_ok = (kc[None, :] >= cs[:, None]) & (kc[None, :] < cs[:, None] + NA_KW)
    dc = kc[None, :] - qc[:, None] + NA_KW - 1
    col_oh = (dc[None] == np.arange(2 * NA_KW - 1)[:, None, None]) & col_ok[None]
    t1 = jnp.einsum("hij,jqk->hiqk", rpb, jnp.asarray(col_oh.astype(np.float32)), precision=lax.Precision.HIGHEST)
    row_ohs, masks = [], []
    for blk in (0, min(1, nblk - 1), nblk - 1):
        r0 = blk * r
        lo = int(np.clip(r0 - NA_KH // 2, 0, rows - nkr))
        qr = r0 + np.arange(r)
        kr = lo + np.arange(nkr)
        rs = np.clip(qr - NA_KH // 2, 0, rows - NA_KH)
        row_ok = (kr[None, :] >= rs[:, None]) & (kr[None, :] < rs[:, None] + NA_KH)
        dr = kr[None, :] - qr[:, None] + NA_KH - 1
        row_ohs.append(((dr[None] == np.arange(2 * NA_KH - 1)[:, None, None]) & row_ok[None]).astype(np.float32))
        ok = row_ok[:, None, :, None] & col_ok[None, :, None, :]
        masks.append(np.where(ok, 0.0, NEG).astype(np.float32).reshape(r * GRID_W, nkr * GRID_W))
    eb = jnp.einsum("viab,hiqk->vhaqbk", jnp.asarray(np.stack(row_ohs)), t1, precision=lax.Precision.HIGHEST)
    eb = eb.reshape(3, NA_HEADS, r * GRID_W, nkr * GRID_W)
    return eb + jnp.asarray(np.stack(masks))[:, None]


def _layer_norm(x, g, b):
    xc = x - jnp.mean(x, axis=-1, keepdims=True)
    return xc * lax.rsqrt(jnp.mean(xc * xc, axis=-1, keepdims=True) + NORM_EPS) * g + b


def _post_kernel(hc, x_ref, of_ref, ob_ref, sg_ref, gq_ref, na_ref, mod_ref, hn_ref, bd_ref, wo_ref, w1_ref, w2_ref,
                 ln_ref, o_ref):
    x = x_ref[0]
    g1, sh2, sc2, g2 = (mod_ref[0, k:k + 1, :] for k in range(4))
    o = of_ref[0] + ob_ref[0]
    ms = _dot(o * o, bd_ref[...])
    hg = (o * lax.rsqrt(ms + NORM_EPS) * hn_ref[...] * sg_ref[0]).astype(BF16)
    y = (_dot(hg, wo_ref[0:HG_W, :]) + _dot(gq_ref[0], wo_ref[HG_W:HG_W + GQA_W, :])
         + _dot(na_ref[0], wo_ref[HG_W + GQA_W:, :]))
    x1 = _layer_norm(DN_ALPHA * x + g1 * y, ln_ref[0:1, :], ln_ref[1:2, :])
    h2 = (x1 * (1.0 + sc2) + sh2).astype(BF16)
    f = None
    for c in range(FFN_HIDDEN // hc):
        ug = _dot(h2, w1_ref[:, c * hc:(c + 1) * hc])
        uu = _dot(h2, w1_ref[:, FFN_HIDDEN + c * hc:FFN_HIDDEN + (c + 1) * hc])
        a = (ug * _sigmoid(ug) * uu).astype(BF16)
        part = _dot(a, w2_ref[c * hc:(c + 1) * hc, :])
        f = part if f is None else f + part
    o_ref[0] = _layer_norm(DN_ALPHA * x1 + g2 * f, ln_ref[2:3, :], ln_ref[3:4, :])


def _post(x, o_f, o_b, sg, gq, na, mod4, hn, bd256, w_out, w1, w2, ln4, tm, hc):
    b, t, d = x.shape
    row = lambda bi, i: (bi, i, 0)
    const2 = lambda bi, i: (0, 0)
    return pl.pallas_call(
        functools.partial(_post_kernel, hc),
        grid=(b, t // tm),
        in_specs=[
            pl.BlockSpec((1, tm, d), row),
            pl.BlockSpec((1, tm, HG_W), row),
            pl.BlockSpec((1, tm, HG_W), row),
            pl.BlockSpec((1, tm, HG_W), row),
            pl.BlockSpec((1, tm, GQA_W), row),
            pl.BlockSpec((1, tm, NA_W), row),
            pl.BlockSpec((1, 4, d), lambda bi, i: (bi, 0, 0)),
            pl.BlockSpec((1, HG_W), const2),
            pl.BlockSpec((HG_W, HG_W), const2),
            pl.BlockSpec((d, d), const2),
            pl.BlockSpec((d, 2 * FFN_HIDDEN), const2),
            pl.BlockSpec((FFN_HIDDEN, d), const2),
            pl.BlockSpec((4, d), const2),
        ],
        out_specs=pl.BlockSpec((1, tm, d), row),
        out_shape=jax.ShapeDtypeStruct((b, t, d), F32),
        compiler_params=_cparams(("arbitrary", "arbitrary")),
        name="post",
    )(x, o_f, o_b, sg, gq, na, mod4, hn, bd256, w_out, w1, w2, ln4)


_PERM_MAIN = np.concatenate([np.arange(0, HEAD_DIM, 2), np.arange(1, HEAD_DIM, 2)])
_PERM_PART = np.concatenate([np.arange(1, HEAD_DIM, 2), np.arange(0, HEAD_DIM, 2)])


def _in_ext_columns():
    base = np.arange(5 * HG_W)
    gq0 = 5 * HG_W
    gk0 = gq0 + GQA_W
    gv0 = gk0 + GQA_KV_W
    nq0 = gv0 + GQA_KV_W
    gq_main = np.concatenate([gq0 + h * HEAD_DIM + _PERM_MAIN for h in GQA_SLAB_HEADS])
    gq_part = np.concatenate([gq0 + h * HEAD_DIM + _PERM_PART for h in GQA_SLAB_HEADS])
    gk_main = np.concatenate([gk0 + h * HEAD_DIM + _PERM_MAIN for h in range(GQA_KV_HEADS)])
    gk_part = np.concatenate([gk0 + h * HEAD_DIM + _PERM_PART for h in range(GQA_KV_HEADS)])
    gv = gv0 + np.arange(GQA_KV_W)
    na = nq0 + np.arange(3 * NA_W)
    return np.concatenate([base, gq_main, gk_main, gv, na, gq_part, gk_part])


def _rope_tables(n_tokens):
    t = np.arange(n_tokens)
    row = (t // GRID_W).astype(np.float32)
    col = (t % GRID_W).astype(np.float32)
    n_pairs = HEAD_DIM // 4
    inv_freq = np.exp(-math.log(ROPE_THETA) * np.arange(n_pairs, dtype=np.float32) / n_pairs).astype(np.float32)
    ang = np.concatenate([row[:, None] * inv_freq, col[:, None] * inv_freq], -1).astype(np.float32)
    ang = jnp.asarray(ang)
    c, s = jnp.cos(ang), jnp.sin(ang)
    cos = jnp.tile(jnp.concatenate([c, c], -1), (1, LANE // HEAD_DIM))
    sin = jnp.tile(jnp.concatenate([-s, s], -1), (1, LANE // HEAD_DIM))
    return cos, sin


def _block_diag_mean(width):
    c = np.arange(width)
    return jnp.asarray(((c[:, None] // HEAD_DIM) == (c[None, :] // HEAD_DIM)).astype(np.float32) / HEAD_DIM)


def kernel(x, c, ctx, c_ctx, w_ada, b_ada, w_in, hgrn_lb_logits, hgrn_norm, gqa_q_norm, gqa_k_norm, na_rpb, w_out,
           w_ffn_in, w_ffn_out, ln_g, ln_b):
    b, n, d = x.shape
    n_ctx = ctx.shape[1]
    rows = n // GRID_W
    tm = min(512, n)
    tm_ctx = min(256, n_ctx)
    tb = 256
    tq, tk = min(256, n), min(1024, n)
    na_r = 4
    hc = FFN_HIDDEN // 2

    cv = jnp.zeros((MOD_ROWS, d), F32).at[:b].set(c).at[b].set(c_ctx)
    mods = _mods(cv, w_ada, b_ada)

    cols = _in_ext_columns()
    w_ext = w_in[:, :, cols].astype(BF16)
    out_rows = np.concatenate(
        [np.arange(HG_W)] + [HG_W + h * HEAD_DIM + np.arange(HEAD_DIM) for h in GQA_SLAB_HEADS]
        + [HG_W + GQA_W + np.arange(NA_W)])
    w_out_p = w_out[:, out_rows, :].astype(BF16)
    w1 = w_ffn_in.astype(BF16)
    w2 = w_ffn_out.astype(BF16)

    cos_l, sin_l = _rope_tables(n)
    cos_c = jnp.ones((n_ctx, LANE), F32)
    sin_c = jnp.zeros((n_ctx, LANE), F32)
    bd128 = _block_diag_mean(LANE)
    bd256 = _block_diag_mean(HG_W)
    hg_consts = _hgrn_consts(tb)
    s_zero = jnp.zeros((b, HG_W, HG_W), F32)

    x_lat, x_ctx = x, ctx
    for l in range(DEPTH):
        need_ctx = l < DEPTH - 1
        m = mods[l].reshape(MOD_ROWS, N_MOD, d)
        m_lat = m[:b]
        m_ctx = jnp.broadcast_to(m[b][None], (b, N_MOD, d))
        qscale = HEAD_DIM ** -0.5 * math.log2(math.e)
        gains = (
            (jnp.tile(gqa_q_norm[l][_PERM_MAIN], GQA_HEADS) * qscale)[None],
            (jnp.tile(gqa_q_norm[l][_PERM_PART], GQA_HEADS) * qscale)[None],
            jnp.tile(gqa_k_norm[l][_PERM_MAIN], GQA_KV_HEADS)[None],
            jnp.tile(gqa_k_norm[l][_PERM_PART], GQA_KV_HEADS)[None],
        )
        p_lat = _inproj(l, x_lat, m_lat[:, 0:1], m_lat[:, 1:2], w_ext[l], hgrn_lb_logits, cos_l, sin_l, gains, bd128, tm)
        p_ctx = _inproj(l, x_ctx, m_ctx[:, 0:1], m_ctx[:, 1:2], w_ext[l], hgrn_lb_logits, cos_c, sin_c, gains, bd128,
                        tm_ctx)
        hq_l, lf_l, hi_l, sg_l, gq_l, gk_l, gv_l, nq_l, nk_l, nv_l, gvt_l = p_lat
        hq_c, lf_c, hi_c, sg_c, gq_c, gk_c, gv_c, nq_c, nk_c, nv_c, gvt_c = p_ctx

        o_c, o_l = [], []
        for direction in range(2):
            oc, s_c = _hgrn(hq_c, lf_c, hi_c, s_zero, direction, hg_consts[direction], min(tb, n_ctx))
            ol, _ = _hgrn(hq_l, lf_l, hi_l, s_c, direction, hg_consts[direction], tb)
            o_c.append(oc)
            o_l.append(ol)

        gqa_l = _gqa(gq_l, gk_l, gvt_l, gk_c, gvt_c, tq, tk)
        eb = _na_bias_tables(na_rpb[l], na_r, rows)
        na_l = _na(nq_l, nk_l, nv_l, nk_c, nv_c, eb, na_r)

        hn = jnp.tile(hgrn_norm[l], HG_HEADS)[None]
        ln4 = jnp.stack([ln_g[l, 0], ln_b[l, 0], ln_g[l, 1], ln_b[l, 1]])
        if need_ctx:
            gqa_c = _ctxattn(gq_c, gk_c, gv_c, True)
            na_c = _ctxattn(nq_c, nk_c, nv_c, False)
            x_ctx = _post(x_ctx, o_c[0], o_c[1], sg_c, gqa_c, na_c, m_ctx[:, 2:6], hn, bd256, w_out_p[l], w1[l], w2[l],
                          ln4, tm_ctx, hc)
        x_lat = _post(x_lat, o_l[0], o_l[1], sg_l, gqa_l, na_l, m_lat[:, 2:6], hn, bd256, w_out_p[l], w1[l], w2[l],
                      ln4, tm, hc)
    return x_lat
```

```python
import functools
import math

import numpy as np
import jax
import jax.numpy as jnp
from jax import lax
from jax.experimental import pallas as pl
from jax.experimental.pallas import tpu as pltpu

F32 = jnp.float32
BF16 = jnp.bfloat16

D_MODEL = 1024
DEPTH = 2
GRID_W = 64
HEAD_DIM = 64
HG_HEADS = 4
GQA_HEADS = 6
GQA_KV_HEADS = 2
NA_HEADS = 6
NA_KH = 8
NA_KW = 16
ROPE_THETA = 10000.0
NORM_EPS = 1e-6
N_MOD = 6
HG_W = HG_HEADS * HEAD_DIM
GQA_W = GQA_HEADS * HEAD_DIM
GQA_KV_W = GQA_KV_HEADS * HEAD_DIM
NA_W = NA_HEADS * HEAD_DIM
FFN_HIDDEN = 2816
DN_ALPHA = (2 * DEPTH) ** 0.25

LANE = 128
SUB = 16
MOD_ROWS = 8
NEG = -1e30
VMEM_LIMIT = 56 * 1024 * 1024

C_HQ, C_FF, C_FB, C_HI, C_HGATE = 0, 256, 512, 768, 1024
C_GQ, C_GK, C_GV = 1280, 1664, 1792
C_NQ, C_NK, C_NV = 1920, 2304, 2688
C_GQP, C_GKP = 3072, 3456
IN_EXT = 3584
GQA_SLAB_HEADS = (0, 3, 1, 4, 2, 5)


def _cparams(sem, flags=None):
    return pltpu.CompilerParams(dimension_semantics=sem, vmem_limit_bytes=VMEM_LIMIT, flags=flags)


def _sigmoid(z):
    return 1.0 / (1.0 + jnp.exp(-z))


def _dot(a, b):
    return jnp.dot(a, b, preferred_element_type=F32)


def _dot_nt(a, b):
    return lax.dot_general(a, b, (((1,), (1,)), ((), ())), preferred_element_type=F32)


def _mods_kernel(cv_ref, w_ref, b_ref, o_ref):
    cv = cv_ref[...]
    a = cv * _sigmoid(cv)
    o_ref[0] = jnp.dot(a, w_ref[0], preferred_element_type=F32, precision=lax.Precision.HIGHEST) + b_ref[0]


def _mods(cv, w_ada, b_ada):
    depth, d, nm = w_ada.shape
    tn = 1024
    return pl.pallas_call(
        _mods_kernel,
        grid=(depth, nm // tn),
        in_specs=[
            pl.BlockSpec((MOD_ROWS, d), lambda l, j: (0, 0)),
            pl.BlockSpec((1, d, tn), lambda l, j: (l, 0, j)),
            pl.BlockSpec((1, 1, tn), lambda l, j: (l, 0, j)),
        ],
        out_specs=pl.BlockSpec((1, MOD_ROWS, tn), lambda l, j: (l, 0, j)),
        out_shape=jax.ShapeDtypeStruct((depth, MOD_ROWS, nm), F32),
        compiler_params=_cparams(("arbitrary", "arbitrary")),
        name="mods",
    )(cv, w_ada, b_ada.reshape(depth, 1, nm))


def _inproj_kernel(layer, x_ref, sh_ref, sc_ref, w_ref, lbl_ref, cos_ref, sin_ref, gq_ref, gqp_ref, gk_ref, gkp_ref,
                   bd_ref, hq_ref, lf_ref, hi_ref, hg_ref, gq_o, gk_o, gv_o, nq_o, nk_o, nv_o, gvt_o):
    x = x_ref[0]
    h = (x * (1.0 + sc_ref[0]) + sh_ref[0]).astype(BF16)

    def proj(c0, width):
        return _dot(h, w_ref[:, c0:c0 + width])

    logits = [lbl_ref[d] for d in range(DEPTH)]
    mx = functools.reduce(jnp.maximum, logits)
    es = [jnp.exp(v - mx) for v in logits]
    inv = 1.0 / functools.reduce(lambda a, b: a + b, es)
    ps = [e * inv for e in es]
    lb = jnp.clip(functools.reduce(lambda a, b: a + b, ps[:layer + 1]) - ps[0], 0.0, 1.0)

    def put_halves(ref, first, a):
        for c in range(HG_W // LANE):
            ref[0, first + c] = a[:, c * LANE:(c + 1) * LANE]

    q = proj(C_HQ, HG_W)
    put_halves(hq_ref, 0, q * _sigmoid(q))
    for d, c0 in enumerate((C_FF, C_FB)):
        z = proj(c0, HG_W)
        lbd = lb[d:d + 1, :]
        put_halves(lf_ref, d * (HG_W // LANE), jnp.log(lbd + (1.0 - lbd) * _sigmoid(z)))
    put_halves(hi_ref, 0, proj(C_HI, HG_W))
    g = proj(C_HGATE, HG_W)
    hg_ref[0] = g * _sigmoid(g)

    cos = cos_ref[...]
    sin = sin_ref[...]
    bd = bd_ref[...]

    def normed_rope(c_main, c_part, gm_ref, gp_ref, out_ref, nslab):
        for s in range(nslab):
            pm = proj(c_main + s * LANE, LANE)
            pp = proj(c_part + s * LANE, LANE)
            ms = _dot((pm * pm).astype(BF16), bd)
            r = lax.rsqrt(ms + NORM_EPS)
            gm = gm_ref[:, s * LANE:(s + 1) * LANE]
            gp = gp_ref[:, s * LANE:(s + 1) * LANE]
            out_ref[0, :, s * LANE:(s + 1) * LANE] = ((pm * r * gm) * cos + (pp * r * gp) * sin).astype(BF16)

    normed_rope(C_GQ, C_GQP, gq_ref, gqp_ref, gq_o, GQA_W // LANE)
    normed_rope(C_GK, C_GKP, gk_ref, gkp_ref, gk_o, GQA_KV_W // LANE)
    gv = proj(C_GV, GQA_KV_W)
    gv_o[0] = gv.astype(BF16)
    lane = lax.broadcasted_iota(jnp.int32, gv.shape, 1)
    gvt_o[0, 0] = jnp.where(lane < HEAD_DIM, gv, jnp.where(lane == HEAD_DIM, 1.0, 0.0)).T.astype(BF16)
    gvt_o[0, 1] = jnp.where(lane >= HEAD_DIM, gv, jnp.where(lane == 0, 1.0, 0.0)).T.astype(BF16)
    nq_o[0] = (proj(C_NQ, NA_W) * (HEAD_DIM ** -0.5)).astype(BF16)
    nk_o[0] = proj(C_NK, NA_W).astype(BF16)
    nv_o[0] = proj(C_NV, NA_W).astype(BF16)


def _inproj(layer, x, shift, scale, w_ext, lb_logits, cos, sin, gains, bd128, tm):
    b, t, d = x.shape
    gq, gqp, gk, gkp = gains
    row = lambda bi, i: (bi, i, 0)
    const2 = lambda bi, i: (0, 0)
    halves = (HG_W // LANE, 2 * HG_W // LANE, HG_W // LANE)
    widths = (HG_W, GQA_W, GQA_KV_W, GQA_KV_W, NA_W, NA_W, NA_W)
    dtypes = (F32, BF16, BF16, BF16, BF16, BF16, BF16)
    return pl.pallas_call(
        functools.partial(_inproj_kernel, layer),
        grid=(b, t // tm),
        in_specs=[
            pl.BlockSpec((1, tm, d), row),
            pl.BlockSpec((1, 1, d), lambda bi, i: (bi, 0, 0)),
            pl.BlockSpec((1, 1, d), lambda bi, i: (bi, 0, 0)),
            pl.BlockSpec((d, IN_EXT), const2),
            pl.BlockSpec((DEPTH, 2, HG_W), lambda bi, i: (0, 0, 0)),
            pl.BlockSpec((tm, LANE), lambda bi, i: (i, 0)),
            pl.BlockSpec((tm, LANE), lambda bi, i: (i, 0)),
            pl.BlockSpec((1, GQA_W), const2),
            pl.BlockSpec((1, GQA_W), const2),
            pl.BlockSpec((1, GQA_KV_W), const2),
            pl.BlockSpec((1, GQA_KV_W), const2),
            pl.BlockSpec((LANE, LANE), const2),
        ],
        out_specs=[pl.BlockSpec((1, nh, tm, LANE), lambda bi, i: (bi, 0, i, 0)) for nh in halves]
        + [pl.BlockSpec((1, tm, w), row) for w in widths]
        + [pl.BlockSpec((1, 2, LANE, tm), lambda bi, i: (bi, 0, 0, i))],
        out_shape=[jax.ShapeDtypeStruct((b, nh, t, LANE), F32) for nh in halves]
        + [jax.ShapeDtypeStruct((b, t, w), dt) for w, dt in zip(widths, dtypes)]
        + [jax.ShapeDtypeStruct((b, 2, LANE, t), BF16)],
        compiler_params=_cparams(("arbitrary", "arbitrary")),
        name="inproj",
    )(x, shift, scale, w_ext, lb_logits, cos, sin, gq, gqp, gk, gkp, bd128)


def _hgrn_kernel(tb, qf_ref, lff_ref, vf_ref, qb_ref, lfb_ref, vb_ref, s0_ref, ind_ref, bm_ref,
                 of_ref, ob_ref, sfin_ref, st_scr, stb_scr, qs_scr, ks_scr):
    i = pl.program_id(1)
    nsc = tb // SUB

    @pl.when(i == 0)
    def _():
        st_scr[...] = s0_ref[0]
        stb_scr[...] = s0_ref[0].astype(BF16)

    ind = ind_ref[...]
    in_refs = ((qf_ref, lff_ref, vf_ref), (qb_ref, lfb_ref, vb_ref))
    o_refs = (of_ref, ob_ref)

    nh = HG_W // LANE

    def slab(tt):
        return pl.ds(tt, nsc, stride=SUB)

    def load(ref, lead, rows):
        return jnp.concatenate([ref[(*lead, c, rows, slice(None))] for c in range(nh)], axis=1)

    def store(ref, lead, rows, a, add=False):
        for c in range(nh):
            idx = (*lead, c, rows, slice(None))
            piece = a[:, c * LANE:(c + 1) * LANE]
            ref[idx] = ref[idx] + piece if add else piece

    orders = (list(range(SUB)), list(range(SUB - 1, -1, -1)))
    q, v, f, k, dec = [], [], [], [], []
    for d, (q_ref, lf_ref, v_ref) in enumerate(in_refs):
        lf = {tt: load(lf_ref, (0,), slab(tt)) for tt in orders[d]}
        q.append({tt: load(q_ref, (0,), slab(tt)) for tt in orders[d]})
        v.append({tt: load(v_ref, (0,), slab(tt)) for tt in orders[d]})
        f.append({tt: jnp.exp(lf[tt]) for tt in orders[d]})
        k.append({tt: 1.0 - f[d][tt] for tt in orders[d]})
        cum, run = {}, None
        for tt in orders[d]:
            run = lf[tt] if run is None else run + lf[tt]
            cum[tt] = run
        for tt in orders[d]:
            store(qs_scr, (d,), slab(tt), q[d][tt] * jnp.exp(cum[tt]))
            store(ks_scr, (d,), slab(tt), k[d][tt] * jnp.exp(run - cum[tt]))
        dec.append(jnp.exp(run))

    acc = [dict(), dict()]
    for a in range(SUB):
        for d in range(2):
            ss = orders[d][a]
            u = k[d][ss]
            es = []
            for tt in orders[d][a:]:
                if tt != ss:
                    u = u * f[d][tt]
                es.append((q[d][tt] * u).astype(BF16))
            att = _dot(jnp.concatenate(es, axis=0), ind)
            for n, tt in enumerate(orders[d][a:]):
                part = att[n * nsc:(n + 1) * nsc] * v[d][ss]
                acc[d][tt] = part if a == 0 else acc[d][tt] + part
    for d in range(2):
        for tt in orders[d]:
            store(o_refs[d], (0,), slab(tt), acc[d][tt])

    bm = bm_ref[...]
    for step in range(nsc):
        for d in range(2):
            j = nsc - 1 - step if d else step
            sl = slice(j * SUB, (j + 1) * SUB)
            store(o_refs[d], (0,), sl, _dot_nt(load(qs_scr, (d,), sl).astype(BF16), stb_scr[d]), add=True)
            kvt = lax.dot_general(load(in_refs[d][2], (0,), sl).astype(BF16), load(ks_scr, (d,), sl).astype(BF16),
                                  (((0,), (0,)), ((), ())), preferred_element_type=F32)
            for h in range(HG_HEADS):
                rs = slice(h * HEAD_DIM, (h + 1) * HEAD_DIM)
                ls = slice((h // 2) * LANE, (h // 2 + 1) * LANE)
                blk = st_scr[d, rs, ls] * dec[d][j:j + 1, ls] + kvt[rs, ls] * bm[rs, ls]
                st_scr[d, rs, ls] = blk
                stb_scr[d, rs, ls] = blk.astype(BF16)

    @pl.when(i == pl.num_programs(1) - 1)
    def _():
        sfin_ref[0] = st_scr[...]


def _hgrn(q, lf2, v, s0, consts, tb):
    b, nh, t, _ = q.shape
    nblk = t // tb
    ind, bm = consts
    fwd = lambda bi, i: (bi, 0, i, 0)
    bwd = lambda bi, i: (bi, 0, nblk - 1 - i, 0)
    bwd_lf = lambda bi, i: (bi, 1, nblk - 1 - i, 0)
    const2 = lambda bi, i: (0, 0)
    state = lambda bi, i: (bi, 0, 0, 0)
    blk = (1, nh, tb, LANE)
    return pl.pallas_call(
        functools.partial(_hgrn_kernel, tb),
        grid=(b, nblk),
        in_specs=[
            pl.BlockSpec(blk, fwd), pl.BlockSpec(blk, fwd), pl.BlockSpec(blk, fwd),
            pl.BlockSpec(blk, bwd), pl.BlockSpec(blk, bwd_lf), pl.BlockSpec(blk, bwd),
            pl.BlockSpec((1, 2, HG_W, HG_W), state),
            pl.BlockSpec((HG_W, HG_W), const2),
            pl.BlockSpec((HG_W, HG_W), const2),
        ],
        out_specs=[pl.BlockSpec(blk, fwd), pl.BlockSpec(blk, bwd), pl.BlockSpec((1, 2, HG_W, HG_W), state)],
        out_shape=[
            jax.ShapeDtypeStruct((b, nh, t, LANE), F32),
            jax.ShapeDtypeStruct((b, nh, t, LANE), F32),
            jax.ShapeDtypeStruct((b, 2, HG_W, HG_W), F32),
        ],
        scratch_shapes=[pltpu.VMEM((2, HG_W, HG_W), F32), pltpu.VMEM((2, HG_W, HG_W), BF16),
                        pltpu.VMEM((2, nh, tb, LANE), F32), pltpu.VMEM((2, nh, tb, LANE), F32)],
        compiler_params=_cparams(("arbitrary", "arbitrary")),
        name="hgrn",
    )(q, lf2, v, q, lf2, v, s0, ind, bm)


def _hgrn_consts():
    c = np.arange(HG_W)
    blk = (c[:, None] // HEAD_DIM) == (c[None, :] // HEAD_DIM)
    return jnp.asarray(blk.astype(np.float32), BF16), jnp.asarray(blk.astype(np.float32))


def _half_masks(rows):
    lane = lax.broadcasted_iota(jnp.int32, (rows, LANE), 1)
    return lane < HEAD_DIM, lane >= HEAD_DIM


def _sel(mask, a):
    return jnp.where(mask, a, jnp.zeros_like(a))


def _value_slabs(v):
    lane = lax.broadcasted_iota(jnp.int32, v.shape, 1)
    lo, hi = _half_masks(v.shape[0])
    e_lo = jnp.where(lane == HEAD_DIM, 1.0, 0.0).astype(v.dtype)
    e_hi = jnp.where(lane == 0, 1.0, 0.0).astype(v.dtype)
    return _sel(lo, v) + e_lo, _sel(hi, v) + e_hi


def _normalize_slab(a_lo, a_hi):
    lane = lax.broadcasted_iota(jnp.int32, a_lo.shape, 1)
    return jnp.where(lane < HEAD_DIM, a_lo / a_lo[:, HEAD_DIM:HEAD_DIM + 1], a_hi / a_hi[:, 0:1])


def _gqa_kernel(tq, tk, n_lat, q_ref, kl_ref, vtl_ref, kc_ref, vtc_ref, o_ref, m_scr, acc_scr):
    nslab = GQA_W // LANE
    qlo, qhi = _half_masks(tq)
    qh = []
    for s in range(nslab):
        qs = q_ref[0, :, s * LANE:(s + 1) * LANE]
        qh.append((_sel(qlo, qs), _sel(qhi, qs)))
    m_scr[...] = jnp.full(m_scr.shape, -jnp.inf, F32)
    acc_scr[...] = jnp.zeros(acc_scr.shape, F32)

    heads = [(s, half) for s in range(nslab) for half in range(2)]

    def step(k, vt):
        ahead = 6
        sts = [_dot_nt(k, qh[s][half]) for (s, half) in heads[:ahead]]
        for idx, (s, half) in enumerate(heads):
            st = sts[idx]
            if idx + ahead < len(heads):
                s2, half2 = heads[idx + ahead]
                sts.append(_dot_nt(k, qh[s2][half2]))
            m_old = m_scr[idx]
            m_new = jnp.maximum(m_old, jnp.max(st, axis=0, keepdims=True))
            alpha = jnp.exp2(m_old - m_new)
            pt = jnp.exp2((st - m_new).astype(BF16))
            acc_scr[idx] = alpha * acc_scr[idx] + _dot(vt[half], pt)
            m_scr[idx] = m_new

    def body(c, carry):
        start = pl.multiple_of(c * tk, tk)
        step(kl_ref[0, pl.ds(start, tk), :], [vtl_ref[0, h, :, pl.ds(start, tk)] for h in range(2)])
        return carry

    lax.fori_loop(0, n_lat // tk, body, 0)
    step(kc_ref[0], [vtc_ref[0, h] for h in range(2)])
    row = lax.broadcasted_iota(jnp.int32, (LANE, tq), 0)
    for s in range(nslab):
        a_lo, a_hi = acc_scr[2 * s], acc_scr[2 * s + 1]
        out_t = jnp.where(row < HEAD_DIM, a_lo / a_lo[HEAD_DIM:HEAD_DIM + 1, :], a_hi / a_hi[0:1, :])
        o_ref[0, :, s * LANE:(s + 1) * LANE] = out_t.T.astype(BF16)


def _gqa(q, k_lat, vt_lat, k_ctx, vt_ctx, tq, tk):
    b, t, _ = q.shape
    n_lat = k_lat.shape[1]
    n_ctx = k_ctx.shape[1]
    full = lambda bi, i: (bi, 0, 0)
    full4 = lambda bi, i: (bi, 0, 0, 0)
    return pl.pallas_call(
        functools.partial(_gqa_kernel, tq, tk, n_lat),
        grid=(b, t // tq),
        in_specs=[
            pl.BlockSpec((1, tq, GQA_W), lambda bi, i: (bi, i, 0)),
            pl.BlockSpec((1, n_lat, GQA_KV_W), full),
            pl.BlockSpec((1, 2, LANE, n_lat), full4),
            pl.BlockSpec((1, n_ctx, GQA_KV_W), full),
            pl.BlockSpec((1, 2, LANE, n_ctx), full4),
        ],
        out_specs=pl.BlockSpec((1, tq, GQA_W), lambda bi, i: (bi, i, 0)),
        out_shape=jax.ShapeDtypeStruct((b, t, GQA_W), BF16),
        scratch_shapes=[
            pltpu.VMEM((GQA_HEADS, 1, tq), F32),
            pltpu.VMEM((GQA_HEADS, LANE, tq), F32),
        ],
        compiler_params=_cparams(("arbitrary", "arbitrary")),
        name="gqa",
    )(q, k_lat, vt_lat, k_ctx, vt_ctx)


def _ctxattn_kernel(nslab, shared_kv, q_ref, k_ref, v_ref, o_ref):
    n = q_ref.shape[1]
    lo, hi = _half_masks(n)
    ex = jnp.exp2 if shared_kv else jnp.exp
    for s in range(nslab):
        ks = 0 if shared_kv else s
        q = q_ref[0, :, s * LANE:(s + 1) * LANE]
        k = k_ref[0, :, ks * LANE:(ks + 1) * LANE]
        v = v_ref[0, :, ks * LANE:(ks + 1) * LANE]
        out = None
        for mask in (lo, hi):
            sc = _dot_nt(_sel(mask, q), k)
            p = ex(sc - jnp.max(sc, axis=1, keepdims=True))
            o_h = _dot(p.astype(BF16), _sel(mask, v)) / jnp.sum(p, axis=1, keepdims=True)
            out = o_h if out is None else out + o_h
        o_ref[0, :, s * LANE:(s + 1) * LANE] = out.astype(BF16)


def _ctxattn(q, k, v, shared_kv):
    b, n, w = q.shape
    kw = k.shape[2]
    full = lambda bi: (bi, 0, 0)
    return pl.pallas_call(
        functools.partial(_ctxattn_kernel, w // LANE, shared_kv),
        grid=(b,),
        in_specs=[pl.BlockSpec((1, n, w), full), pl.BlockSpec((1, n, kw), full), pl.BlockSpec((1, n, kw), full)],
        out_specs=pl.BlockSpec((1, n, w), full),
        out_shape=jax.ShapeDtypeStruct((b, n, w), BF16),
        compiler_params=_cparams(("arbitrary",)),
        name="ctxattn",
    )(q, k, v)


def _na_window_start(i, r, rows):
    return jnp.clip(i * r - NA_KH // 2, 0, rows - (r + NA_KH))


def _na_kernel(r, rows, q_ref, k_ref, v_ref, kc_ref, vc_ref, eb_ref, o_ref):
    i = pl.program_id(1)
    tq = r * GRID_W
    kwin = (r + NA_KH) * GRID_W
    start = pl.multiple_of(_na_window_start(i, r, rows) * GRID_W, GRID_W)
    qlo, qhi = _half_masks(tq)
    wlo, whi = _half_masks(kwin)
    clo, chi = _half_masks(kc_ref.shape[1])
    for s in range(NA_W // LANE):
        sl = slice(s * LANE, (s + 1) * LANE)
        q = q_ref[0, :, sl]
        kw = k_ref[0, pl.ds(start, kwin), sl]
        vw = v_ref[0, pl.ds(start, kwin), sl]
        kc = kc_ref[0, :, sl]
        vc = vc_ref[0, :, sl]
        out = None
        for half, (qm, wm, cm) in enumerate(((qlo, wlo, clo), (qhi, whi, chi))):
            qh = _sel(qm, q)
            s_loc = _dot_nt(qh, kw) + eb_ref[0, 2 * s + half]
            s_ctx = _dot_nt(qh, kc)
            m = jnp.maximum(jnp.max(s_loc, axis=1, keepdims=True), jnp.max(s_ctx, axis=1, keepdims=True))
            p_loc = jnp.exp(s_loc - m)
            p_ctx = jnp.exp(s_ctx - m)
            den = jnp.sum(p_loc, axis=1, keepdims=True) + jnp.sum(p_ctx, axis=1, keepdims=True)
            o_h = (_dot(p_loc.astype(BF16), _sel(wm, vw)) + _dot(p_ctx.astype(BF16), _sel(cm, vc))) / den
            out = o_h if out is None else out + o_h
        o_ref[0, :, sl] = out.astype(BF16)


def _na(q, k, v, k_ctx, v_ctx, eb, r):
    b, t, w = q.shape
    rows = t // GRID_W
    nblk = rows // r
    tq = r * GRID_W
    kwin = (r + NA_KH) * GRID_W
    n_ctx = k_ctx.shape[1]
    full = lambda bi, i: (bi, 0, 0)

    def variant(bi, i):
        return (jnp.where(i == 0, 0, jnp.where(i == nblk - 1, 2, 1)), 0, 0, 0)

    return pl.pallas_call(
        functools.partial(_na_kernel, r, rows),
        grid=(b, nblk),
        in_specs=[
            pl.BlockSpec((1, tq, w), lambda bi, i: (bi, i, 0)),
            pl.BlockSpec((1, t, w), full),
            pl.BlockSpec((1, t, w), full),
            pl.BlockSpec((1, n_ctx, w), full),
            pl.BlockSpec((1, n_ctx, w), full),
            pl.BlockSpec((1, NA_HEADS, tq, kwin), variant),
        ],
        out_specs=pl.BlockSpec((1, tq, w), lambda bi, i: (bi, i, 0)),
        out_shape=jax.ShapeDtypeStruct((b, t, w), BF16),
        compiler_params=_cparams(("arbitrary", "arbitrary")),
        name="na",
    )(q, k, v, k_ctx, v_ctx, eb)


def _na_bias_tables(rpb, r, rows):
    nblk = rows // r
    nkr = r + NA_KH
    qc = np.arange(GRID_W)
    kc = np.arange(GRID_W)
    cs = np.clip(qc - NA_KW // 2, 0, GRID_W - NA_KW)
    col_ok = (kc[None, :] >= cs[:, None]) & (kc[None, :] < cs[:, None] + NA_KW)
    dc = kc[None, :] - qc[:, None] + NA_KW - 1
    col_oh = (dc[None] == np.arange(2 * NA_KW - 1)[:, None, None]) & col_ok[None]
    t1 = jnp.einsum("hij,jqk->hiqk", rpb, jnp.asarray(col_oh.astype(np.float32)), precision=lax.Precision.HIGHEST)
    t1 = t1 + jnp.asarray(np.where(col_ok, 0.0, NEG).astype(np.float32))
    neg_blk = jnp.full((NA_HEADS, GRID_W, GRID_W), NEG, F32)
    variants = []
    for blk in (0, min(1, nblk - 1), nblk - 1):
        r0 = blk * r
        lo = int(np.clip(r0 - NA_KH // 2, 0, rows - nkr))
        rows_out = []
        for a in range(r):
            qr = r0 + a
            rs = int(np.clip(qr - NA_KH // 2, 0, rows - NA_KH))
            blks = []
            for bk in range(nkr):
                kr = lo + bk
                blks.append(t1[:, kr - qr + NA_KH - 1] if rs <= kr < rs + NA_KH else neg_blk)
            rows_out.append(jnp.concatenate(blks, axis=-1))
        variants.append(jnp.concatenate(rows_out, axis=1))
    return jnp.stack(variants)


def _layer_norm(x, g, b):
    xc = x - jnp.mean(x, axis=-1, keepdims=True)
    return xc * lax.rsqrt(jnp.mean(xc * xc, axis=-1, keepdims=True) + NORM_EPS) * g + b


def _post_kernel(hc, x_ref, of_ref, ob_ref, sg_ref, gq_ref, na_ref, mod_ref, hn_ref, bd_ref, wo_ref, w1_ref, w2_ref,
                 ln_ref, o_ref):
    x = x_ref[0]
    g1, sh2, sc2, g2 = (mod_ref[0, k:k + 1, :] for k in range(4))
    o = jnp.concatenate([of_ref[0, c] + ob_ref[0, c] for c in range(HG_W // LANE)], axis=1)
    ms = _dot((o * o).astype(BF16), bd_ref[...])
    hg = (o * lax.rsqrt(ms + NORM_EPS) * hn_ref[...] * sg_ref[0]).astype(BF16)
    y = (_dot(hg, wo_ref[0:HG_W, :]) + _dot(gq_ref[0], wo_ref[HG_W:HG_W + GQA_W, :])
         + _dot(na_ref[0], wo_ref[HG_W + GQA_W:, :]))
    x1 = _layer_norm(DN_ALPHA * x + g1 * y, ln_ref[0:1, :], ln_ref[1:2, :])
    h2 = (x1 * (1.0 + sc2) + sh2).astype(BF16)
    f = None
    for c in range(FFN_HIDDEN // hc):
        ug = _dot(h2, w1_ref[:, c * hc:(c + 1) * hc])
        uu = _dot(h2, w1_ref[:, FFN_HIDDEN + c * hc:FFN_HIDDEN + (c + 1) * hc])
        a = (ug * _sigmoid(ug) * uu).astype(BF16)
        part = _dot(a, w2_ref[c * hc:(c + 1) * hc, :])
        f = part if f is None else f + part
    o_ref[0] = _layer_norm(DN_ALPHA * x1 + g2 * f, ln_ref[2:3, :], ln_ref[3:4, :])


def _post(x, o_f, o_b, sg, gq, na, mod4, hn, bd256, w_out, w1, w2, ln4, tm, hc):
    b, t, d = x.shape
    row = lambda bi, i: (bi, i, 0)
    const2 = lambda bi, i: (0, 0)
    return pl.pallas_call(
        functools.partial(_post_kernel, hc),
        grid=(b, t // tm),
        in_specs=[
            pl.BlockSpec((1, tm, d), row),
            pl.BlockSpec((1, HG_W // LANE, tm, LANE), lambda bi, i: (bi, 0, i, 0)),
            pl.BlockSpec((1, HG_W // LANE, tm, LANE), lambda bi, i: (bi, 0, i, 0)),
            pl.BlockSpec((1, tm, HG_W), row),
            pl.BlockSpec((1, tm, GQA_W), row),
            pl.BlockSpec((1, tm, NA_W), row),
            pl.BlockSpec((1, 4, d), lambda bi, i: (bi, 0, 0)),
            pl.BlockSpec((1, HG_W), const2),
            pl.BlockSpec((HG_W, HG_W), const2),
            pl.BlockSpec((d, d), const2),
            pl.BlockSpec((d, 2 * FFN_HIDDEN), const2),
            pl.BlockSpec((FFN_HIDDEN, d), const2),
            pl.BlockSpec((4, d), const2),
        ],
        out_specs=pl.BlockSpec((1, tm, d), row),
        out_shape=jax.ShapeDtypeStruct((b, t, d), F32),
        compiler_params=_cparams(("arbitrary", "arbitrary")),
        name="post",
    )(x, o_f, o_b, sg, gq, na, mod4, hn, bd256, w_out, w1, w2, ln4)


_PERM_MAIN = np.concatenate([np.arange(0, HEAD_DIM, 2), np.arange(1, HEAD_DIM, 2)])
_PERM_PART = np.concatenate([np.arange(1, HEAD_DIM, 2), np.arange(0, HEAD_DIM, 2)])


def _w_in_ext(w_in):
    w = w_in.astype(BF16)
    gq0 = 5 * HG_W
    gk0 = gq0 + GQA_W
    gv0 = gk0 + GQA_KV_W

    def head(base, h, odd_first):
        pairs = w[:, :, base + h * HEAD_DIM:base + (h + 1) * HEAD_DIM].reshape(w.shape[0], w.shape[1], HEAD_DIM // 2, 2)
        even, odd = pairs[..., 0], pairs[..., 1]
        return jnp.concatenate([odd, even] if odd_first else [even, odd], axis=-1)

    cols = [w[:, :, :gq0]]
    cols += [head(gq0, h, False) for h in GQA_SLAB_HEADS] + [head(gk0, h, False) for h in range(GQA_KV_HEADS)]
    cols += [w[:, :, gv0:]]
    cols += [head(gq0, h, True) for h in GQA_SLAB_HEADS] + [head(gk0, h, True) for h in range(GQA_KV_HEADS)]
    return jnp.concatenate(cols, axis=-1)


def _w_out_rows(w_out):
    g0 = HG_W
    blocks = [w_out[:, :g0]] + [w_out[:, g0 + h * HEAD_DIM:g0 + (h + 1) * HEAD_DIM] for h in GQA_SLAB_HEADS]
    return jnp.concatenate(blocks + [w_out[:, g0 + GQA_W:]], axis=1).astype(BF16)


def _rope_tables(n_tokens):
    t = np.arange(n_tokens)
    row = (t // GRID_W).astype(np.float32)
    col = (t % GRID_W).astype(np.float32)
    n_pairs = HEAD_DIM // 4
    inv_freq = np.exp(-math.log(ROPE_THETA) * np.arange(n_pairs, dtype=np.float32) / n_pairs).astype(np.float32)
    ang = np.concatenate([row[:, None] * inv_freq, col[:, None] * inv_freq], -1).astype(np.float32)
    ang = jnp.asarray(ang)
    c, s = jnp.cos(ang), jnp.sin(ang)
    cos = jnp.tile(jnp.concatenate([c, c], -1), (1, LANE // HEAD_DIM))
    sin = jnp.tile(jnp.concatenate([-s, s], -1), (1, LANE // HEAD_DIM))
    return cos, sin


def _block_diag_mean(width):
    c = np.arange(width)
    return jnp.asarray(((c[:, None] // HEAD_DIM) == (c[None, :] // HEAD_DIM)).astype(np.float32) / HEAD_DIM, BF16)


def kernel(x, c, ctx, c_ctx, w_ada, b_ada, w_in, hgrn_lb_logits, hgrn_norm, gqa_q_norm, gqa_k_norm, na_rpb, w_out,
           w_ffn_in, w_ffn_out, ln_g, ln_b):
    b, n, d = x.shape
    n_ctx = ctx.shape[1]
    rows = n // GRID_W
    tm = min(512, n)
    tm_ctx = min(256, n_ctx)
    tb = 256
    tq, tk = min(256, n), min(1024, n)
    na_r = 4
    hc = FFN_HIDDEN // 2

    cv = jnp.zeros((MOD_ROWS, d), F32).at[:b].set(c).at[b].set(c_ctx)
    mods = _mods(cv, w_ada, b_ada)

    w_ext = _w_in_ext(w_in)
    w_out_p = _w_out_rows(w_out)
    w1 = w_ffn_in.astype(BF16)
    w2 = w_ffn_out.astype(BF16)

    cos_l, sin_l = _rope_tables(n)
    cos_c = jnp.ones((n_ctx, LANE), F32)
    sin_c = jnp.zeros((n_ctx, LANE), F32)
    bd128 = _block_diag_mean(LANE)
    bd256 = _block_diag_mean(HG_W)
    hg_consts = _hgrn_consts()
    s_zero = jnp.zeros((b, 2, HG_W, HG_W), F32)

    x_lat, x_ctx = x, ctx
    for l in range(DEPTH):
        need_ctx = l < DEPTH - 1
        m = mods[l].reshape(MOD_ROWS, N_MOD, d)
        m_lat = m[:b]
        m_ctx = jnp.broadcast_to(m[b][None], (b, N_MOD, d))
        qscale = HEAD_DIM ** -0.5 * math.log2(math.e)
        gains = (
            (jnp.tile(gqa_q_norm[l][_PERM_MAIN], GQA_HEADS) * qscale)[None],
            (jnp.tile(gqa_q_norm[l][_PERM_PART], GQA_HEADS) * qscale)[None],
            jnp.tile(gqa_k_norm[l][_PERM_MAIN], GQA_KV_HEADS)[None],
            jnp.tile(gqa_k_norm[l][_PERM_PART], GQA_KV_HEADS)[None],
        )
        p_lat = _inproj(l, x_lat, m_lat[:, 0:1], m_lat[:, 1:2], w_ext[l], hgrn_lb_logits, cos_l, sin_l, gains, bd128, tm)
        p_ctx = _inproj(l, x_ctx, m_ctx[:, 0:1], m_ctx[:, 1:2], w_ext[l], hgrn_lb_logits, cos_c, sin_c, gains, bd128,
                        tm_ctx)
        hq_l, lf_l, hi_l, sg_l, gq_l, gk_l, gv_l, nq_l, nk_l, nv_l, gvt_l = p_lat
        hq_c, lf_c, hi_c, sg_c, gq_c, gk_c, gv_c, nq_c, nk_c, nv_c, gvt_c = p_ctx

        *o_c, s_c = _hgrn(hq_c, lf_c, hi_c, s_zero, hg_consts, min(tb, n_ctx))
        *o_l, _ = _hgrn(hq_l, lf_l, hi_l, s_c, hg_consts, tb)

        gqa_l = _gqa(gq_l, gk_l, gvt_l, gk_c, gvt_c, tq, tk)
        eb = _na_bias_tables(na_rpb[l], na_r, rows)
        na_l = _na(nq_l, nk_l, nv_l, nk_c, nv_c, eb, na_r)

        hn = jnp.tile(hgrn_norm[l], HG_HEADS)[None]
        ln4 = jnp.stack([ln_g[l, 0], ln_b[l, 0], ln_g[l, 1], ln_b[l, 1]])
        if need_ctx:
            gqa_c = _ctxattn(gq_c, gk_c, gv_c, True)
            na_c = _ctxattn(nq_c, nk_c, nv_c, False)
            x_ctx = _post(x_ctx, o_c[0], o_c[1], sg_c, gqa_c, na_c, m_ctx[:, 2:6], hn, bd256, w_out_p[l], w1[l], w2[l],
                          ln4, tm_ctx, hc)
        x_lat = _post(x_lat, o_l[0], o_l[1], sg_l, gqa_l, na_l, m_lat[:, 2:6], hn, bd256, w_out_p[l], w1[l], w2[l],
                      ln4, tm, hc)
    return x_lat
```

```python
import functools
import math

import numpy as np
import jax
import jax.numpy as jnp
from jax import lax
from jax.experimental import pallas as pl
from jax.experimental.pallas import tpu as pltpu

F32 = jnp.float32
BF16 = jnp.bfloat16

D_MODEL = 1024
DEPTH = 2
GRID_W = 64
HEAD_DIM = 64
HG_HEADS = 4
GQA_HEADS = 6
GQA_KV_HEADS = 2
NA_HEADS = 6
NA_KH = 8
NA_KW = 16
ROPE_THETA = 10000.0
NORM_EPS = 1e-6
N_MOD = 6
HG_W = HG_HEADS * HEAD_DIM
GQA_W = GQA_HEADS * HEAD_DIM
GQA_KV_W = GQA_KV_HEADS * HEAD_DIM
NA_W = NA_HEADS * HEAD_DIM
FFN_HIDDEN = 2816
DN_ALPHA = (2 * DEPTH) ** 0.25

LANE = 128
SUB = 16
MOD_ROWS = 8
NEG = -1e30
GQA_SHIFT_LIMIT = 100.0
VMEM_LIMIT = 56 * 1024 * 1024

C_HQ, C_FF, C_FB, C_HI, C_HGATE = 0, 256, 512, 768, 1024
C_GQ, C_GK, C_GV = 1280, 1664, 1792
C_NQ, C_NK, C_NV = 1920, 2304, 2688
C_GQP, C_GKP = 3072, 3456
IN_EXT = 3584
GQA_SLAB_HEADS = (0, 3, 1, 4, 2, 5)


def _cparams(sem, flags=None):
    return pltpu.CompilerParams(dimension_semantics=sem, vmem_limit_bytes=VMEM_LIMIT, flags=flags)


def _sigmoid(z):
    return 1.0 / (1.0 + jnp.exp(-z))


def _dot(a, b):
    return jnp.dot(a, b, preferred_element_type=F32)


def _dot_nt(a, b):
    return lax.dot_general(a, b, (((1,), (1,)), ((), ())), preferred_element_type=F32)


def _mods_kernel(cv_ref, w_ref, b_ref, o_ref):
    cv = cv_ref[...]
    a = cv * _sigmoid(cv)
    o_ref[0] = jnp.dot(a, w_ref[0], preferred_element_type=F32, precision=lax.Precision.HIGHEST) + b_ref[0]


def _mods(cv, w_ada, b_ada):
    depth, d, nm = w_ada.shape
    tn = 1024
    return pl.pallas_call(
        _mods_kernel,
        grid=(depth, nm // tn),
        in_specs=[
            pl.BlockSpec((MOD_ROWS, d), lambda l, j: (0, 0)),
            pl.BlockSpec((1, d, tn), lambda l, j: (l, 0, j)),
            pl.BlockSpec((1, 1, tn), lambda l, j: (l, 0, j)),
        ],
        out_specs=pl.BlockSpec((1, MOD_ROWS, tn), lambda l, j: (l, 0, j)),
        out_shape=jax.ShapeDtypeStruct((depth, MOD_ROWS, nm), F32),
        compiler_params=_cparams(("arbitrary", "arbitrary")),
        name="mods",
    )(cv, w_ada, b_ada.reshape(depth, 1, nm))


def _inproj_kernel(layer, x_ref, sh_ref, sc_ref, w_ref, lbl_ref, cos_ref, sin_ref, gq_ref, gqp_ref, gk_ref, gkp_ref,
                   bd_ref, hq_ref, lf_ref, hi_ref, hg_ref, gq_o, gk_o, gv_o, nq_o, nk_o, nv_o, gvt_o, gk2_o):
    x = x_ref[0]
    h = (x * (1.0 + sc_ref[0]) + sh_ref[0]).astype(BF16)

    def proj(c0, width):
        return _dot(h, w_ref[:, c0:c0 + width])

    logits = [lbl_ref[d] for d in range(DEPTH)]
    mx = functools.reduce(jnp.maximum, logits)
    es = [jnp.exp(v - mx) for v in logits]
    inv = 1.0 / functools.reduce(lambda a, b: a + b, es)
    ps = [e * inv for e in es]
    lb = jnp.clip(functools.reduce(lambda a, b: a + b, ps[:layer + 1]) - ps[0], 0.0, 1.0)

    def put_halves(ref, first, a):
        for c in range(HG_W // LANE):
            ref[0, first + c] = a[:, c * LANE:(c + 1) * LANE]

    q = proj(C_HQ, HG_W)
    put_halves(hq_ref, 0, q * _sigmoid(q))
    for d, c0 in enumerate((C_FF, C_FB)):
        z = proj(c0, HG_W)
        lbd = lb[d:d + 1, :]
        put_halves(lf_ref, d * (HG_W // LANE), jnp.log(lbd + (1.0 - lbd) * _sigmoid(z)))
    put_halves(hi_ref, 0, proj(C_HI, HG_W))
    g = proj(C_HGATE, HG_W)
    hg_ref[0] = g * _sigmoid(g)

    cos = cos_ref[...]
    sin = sin_ref[...]
    bd = bd_ref[...]

    def normed_rope(c_main, c_part, gm_ref, gp_ref, out_ref, nslab):
        for s in range(nslab):
            pm = proj(c_main + s * LANE, LANE)
            pp = proj(c_part + s * LANE, LANE)
            ms = _dot((pm * pm).astype(BF16), bd)
            r = lax.rsqrt(ms + NORM_EPS)
            gm = gm_ref[:, s * LANE:(s + 1) * LANE]
            gp = gp_ref[:, s * LANE:(s + 1) * LANE]
            out_ref[0, :, s * LANE:(s + 1) * LANE] = ((pm * r * gm) * cos + (pp * r * gp) * sin).astype(BF16)

    normed_rope(C_GQ, C_GQP, gq_ref, gqp_ref, gq_o, GQA_W // LANE)
    normed_rope(C_GK, C_GKP, gk_ref, gkp_ref, gk_o, GQA_KV_W // LANE)
    gv = proj(C_GV, GQA_KV_W)
    gv_o[0] = gv.astype(BF16)
    lane = lax.broadcasted_iota(jnp.int32, gv.shape, 1)
    gvt_o[0, 0] = jnp.where(lane < HEAD_DIM, gv, jnp.where(lane == HEAD_DIM, 1.0, 0.0)).T.astype(BF16)
    gvt_o[0, 1] = jnp.where(lane >= HEAD_DIM, gv, jnp.where(lane == 0, 1.0, 0.0)).T.astype(BF16)
    gk = gk_o[0].astype(F32)
    gk2_o[0, 0] = jnp.where(lane < HEAD_DIM, gk, jnp.where(lane == HEAD_DIM, 1.0, 0.0)).astype(BF16)
    gk2_o[0, 1] = jnp.where(lane >= HEAD_DIM, gk, jnp.where(lane == 0, 1.0, 0.0)).astype(BF16)
    nq_o[0] = (proj(C_NQ, NA_W) * (HEAD_DIM ** -0.5)).astype(BF16)
    nk_o[0] = proj(C_NK, NA_W).astype(BF16)
    nv_o[0] = proj(C_NV, NA_W).astype(BF16)


def _inproj(layer, x, shift, scale, w_ext, lb_logits, cos, sin, gains, bd128, tm):
    b, t, d = x.shape
    gq, gqp, gk, gkp = gains
    row = lambda bi, i: (bi, i, 0)
    const2 = lambda bi, i: (0, 0)
    halves = (HG_W // LANE, 2 * HG_W // LANE, HG_W // LANE)
    widths = (HG_W, GQA_W, GQA_KV_W, GQA_KV_W, NA_W, NA_W, NA_W)
    dtypes = (F32, BF16, BF16, BF16, BF16, BF16, BF16)
    return pl.pallas_call(
        functools.partial(_inproj_kernel, layer),
        grid=(b, t // tm),
        in_specs=[
            pl.BlockSpec((1, tm, d), row),
            pl.BlockSpec((1, 1, d), lambda bi, i: (bi, 0, 0)),
            pl.BlockSpec((1, 1, d), lambda bi, i: (bi, 0, 0)),
            pl.BlockSpec((d, IN_EXT), const2),
            pl.BlockSpec((DEPTH, 2, HG_W), lambda bi, i: (0, 0, 0)),
            pl.BlockSpec((tm, LANE), lambda bi, i: (i, 0)),
            pl.BlockSpec((tm, LANE), lambda bi, i: (i, 0)),
            pl.BlockSpec((1, GQA_W), const2),
            pl.BlockSpec((1, GQA_W), const2),
            pl.BlockSpec((1, GQA_KV_W), const2),
            pl.BlockSpec((1, GQA_KV_W), const2),
            pl.BlockSpec((LANE, LANE), const2),
        ],
        out_specs=[pl.BlockSpec((1, nh, tm, LANE), lambda bi, i: (bi, 0, i, 0)) for nh in halves]
        + [pl.BlockSpec((1, tm, w), row) for w in widths]
        + [pl.BlockSpec((1, 2, LANE, tm), lambda bi, i: (bi, 0, 0, i)),
           pl.BlockSpec((1, 2, tm, LANE), lambda bi, i: (bi, 0, i, 0))],
        out_shape=[jax.ShapeDtypeStruct((b, nh, t, LANE), F32) for nh in halves]
        + [jax.ShapeDtypeStruct((b, t, w), dt) for w, dt in zip(widths, dtypes)]
        + [jax.ShapeDtypeStruct((b, 2, LANE, t), BF16), jax.ShapeDtypeStruct((b, 2, t, LANE), BF16)],
        compiler_params=_cparams(("arbitrary", "arbitrary")),
        name="inproj",
    )(x, shift, scale, w_ext, lb_logits, cos, sin, gq, gqp, gk, gkp, bd128)


def _hgrn_kernel(tb, qf_ref, lff_ref, vf_ref, qb_ref, lfb_ref, vb_ref, s0_ref, ind_ref, bm_ref,
                 of_ref, ob_ref, sfin_ref, st_scr, stb_scr, qs_scr, ks_scr):
    i = pl.program_id(1)
    nsc = tb // SUB

    @pl.when(i == 0)
    def _():
        st_scr[...] = s0_ref[0]
        stb_scr[...] = s0_ref[0].astype(BF16)

    ind = ind_ref[...]
    in_refs = ((qf_ref, lff_ref, vf_ref), (qb_ref, lfb_ref, vb_ref))
    o_refs = (of_ref, ob_ref)

    nh = HG_W // LANE

    def slab(tt):
        return pl.ds(tt, nsc, stride=SUB)

    def load(ref, lead, rows):
        return jnp.concatenate([ref[(*lead, c, rows, slice(None))] for c in range(nh)], axis=1)

    def store(ref, lead, rows, a, add=False):
        for c in range(nh):
            idx = (*lead, c, rows, slice(None))
            piece = a[:, c * LANE:(c + 1) * LANE]
            ref[idx] = ref[idx] + piece if add else piece

    orders = (list(range(SUB)), list(range(SUB - 1, -1, -1)))
    q, v, f, k, dec = [], [], [], [], []
    for d, (q_ref, lf_ref, v_ref) in enumerate(in_refs):
        lf = {tt: load(lf_ref, (0,), slab(tt)) for tt in orders[d]}
        q.append({tt: load(q_ref, (0,), slab(tt)) for tt in orders[d]})
        v.append({tt: load(v_ref, (0,), slab(tt)) for tt in orders[d]})
        f.append({tt: jnp.exp(lf[tt]) for tt in orders[d]})
        k.append({tt: 1.0 - f[d][tt] for tt in orders[d]})
        cum, run = {}, None
        for tt in orders[d]:
            run = lf[tt] if run is None else run + lf[tt]
            cum[tt] = run
        for tt in orders[d]:
            store(qs_scr, (d,), slab(tt), q[d][tt] * jnp.exp(cum[tt]))
            store(ks_scr, (d,), slab(tt), k[d][tt] * jnp.exp(run - cum[tt]))
        dec.append(jnp.exp(run))

    acc = [dict(), dict()]
    for a in range(SUB):
        for d in range(2):
            ss = orders[d][a]
            u = k[d][ss]
            es = []
            for tt in orders[d][a:]:
                if tt != ss:
                    u = u * f[d][tt]
                es.append((q[d][tt] * u).astype(BF16))
            att = _dot(jnp.concatenate(es, axis=0), ind)
            for n, tt in enumerate(orders[d][a:]):
                part = att[n * nsc:(n + 1) * nsc] * v[d][ss]
                acc[d][tt] = part if a == 0 else acc[d][tt] + part
    for d in range(2):
        for tt in orders[d]:
            store(o_refs[d], (0,), slab(tt), acc[d][tt])

    bm = bm_ref[...]
    for step in range(nsc):
        for d in range(2):
            j = nsc - 1 - step if d else step
            sl = slice(j * SUB, (j + 1) * SUB)
            store(o_refs[d], (0,), sl, _dot_nt(load(qs_scr, (d,), sl).astype(BF16), stb_scr[d]), add=True)
            kvt = lax.dot_general(load(in_refs[d][2], (0,), sl).astype(BF16), load(ks_scr, (d,), sl).astype(BF16),
                                  (((0,), (0,)), ((), ())), preferred_element_type=F32)
            for h in range(HG_HEADS):
                rs = slice(h * HEAD_DIM, (h + 1) * HEAD_DIM)
                ls = slice((h // 2) * LANE, (h // 2 + 1) * LANE)
                blk = st_scr[d, rs, ls] * dec[d][j:j + 1, ls] + kvt[rs, ls] * bm[rs, ls]
                st_scr[d, rs, ls] = blk
                stb_scr[d, rs, ls] = blk.astype(BF16)

    @pl.when(i == pl.num_programs(1) - 1)
    def _():
        sfin_ref[0] = st_scr[...]


def _hgrn(q, lf2, v, s0, consts, tb):
    b, nh, t, _ = q.shape
    nblk = t // tb
    ind, bm = consts
    fwd = lambda bi, i: (bi, 0, i, 0)
    bwd = lambda bi, i: (bi, 0, nblk - 1 - i, 0)
    bwd_lf = lambda bi, i: (bi, 1, nblk - 1 - i, 0)
    const2 = lambda bi, i: (0, 0)
    state = lambda bi, i: (bi, 0, 0, 0)
    blk = (1, nh, tb, LANE)
    return pl.pallas_call(
        functools.partial(_hgrn_kernel, tb),
        grid=(b, nblk),
        in_specs=[
            pl.BlockSpec(blk, fwd), pl.BlockSpec(blk, fwd), pl.BlockSpec(blk, fwd),
            pl.BlockSpec(blk, bwd), pl.BlockSpec(blk, bwd_lf), pl.BlockSpec(blk, bwd),
            pl.BlockSpec((1, 2, HG_W, HG_W), state),
            pl.BlockSpec((HG_W, HG_W), const2),
            pl.BlockSpec((HG_W, HG_W), const2),
        ],
        out_specs=[pl.BlockSpec(blk, fwd), pl.BlockSpec(blk, bwd), pl.BlockSpec((1, 2, HG_W, HG_W), state)],
        out_shape=[
            jax.ShapeDtypeStruct((b, nh, t, LANE), F32),
            jax.ShapeDtypeStruct((b, nh, t, LANE), F32),
            jax.ShapeDtypeStruct((b, 2, HG_W, HG_W), F32),
        ],
        scratch_shapes=[pltpu.VMEM((2, HG_W, HG_W), F32), pltpu.VMEM((2, HG_W, HG_W), BF16),
                        pltpu.VMEM((2, nh, tb, LANE), F32), pltpu.VMEM((2, nh, tb, LANE), F32)],
        compiler_params=_cparams(("arbitrary", "arbitrary")),
        name="hgrn",
    )(q, lf2, v, q, lf2, v, s0, ind, bm)


def _hgrn_consts():
    c = np.arange(HG_W)
    blk = (c[:, None] // HEAD_DIM) == (c[None, :] // HEAD_DIM)
    return jnp.asarray(blk.astype(np.float32), BF16), jnp.asarray(blk.astype(np.float32))


def _half_masks(rows):
    lane = lax.broadcasted_iota(jnp.int32, (rows, LANE), 1)
    return lane < HEAD_DIM, lane >= HEAD_DIM


def _sel(mask, a):
    return jnp.where(mask, a, jnp.zeros_like(a))


def _value_slabs(v):
    lane = lax.broadcasted_iota(jnp.int32, v.shape, 1)
    lo, hi = _half_masks(v.shape[0])
    e_lo = jnp.where(lane == HEAD_DIM, 1.0, 0.0).astype(v.dtype)
    e_hi = jnp.where(lane == 0, 1.0, 0.0).astype(v.dtype)
    return _sel(lo, v) + e_lo, _sel(hi, v) + e_hi


def _normalize_slab(a_lo, a_hi):
    lane = lax.broadcasted_iota(jnp.int32, a_lo.shape, 1)
    return jnp.where(lane < HEAD_DIM, a_lo / a_lo[:, HEAD_DIM:HEAD_DIM + 1], a_hi / a_hi[:, 0:1])


def _gqa_kernel(tq, tk, n_lat, q_ref, kl_ref, vtl_ref, kc_ref, vtc_ref, o_ref, kmax_scr, m_scr, acc_scr):
    i = pl.program_id(1)
    nslab = GQA_W // LANE
    n_chunks = n_lat // tk
    heads = [(s, half) for s in range(nslab) for half in range(2)]

    def key_chunk(c):
        start = pl.multiple_of(c * tk, tk)
        return ([kl_ref[0, h, pl.ds(start, tk), :] for h in range(2)],
                [vtl_ref[0, h, :, pl.ds(start, tk)] for h in range(2)])

    @pl.when(i == 0)
    def _():
        for h in range(2):
            def norm2(kb):
                x = kb.astype(F32)
                lane = lax.broadcasted_iota(jnp.int32, x.shape, 1)
                x = jnp.where((lane < HEAD_DIM) if h == 0 else (lane >= HEAD_DIM), x, 0.0)
                mx = jnp.max(jnp.sum(x * x, axis=1, keepdims=True), axis=0, keepdims=True)
                return jnp.broadcast_to(mx, kmax_scr.shape[1:])

            mx = lax.fori_loop(0, n_chunks, lambda c, m: jnp.maximum(m, norm2(key_chunk(c)[0][h])),
                               norm2(kc_ref[0, h]))
            kmax_scr[h] = jnp.sqrt(mx)

    qlo, qhi = _half_masks(tq)
    lane = lax.broadcasted_iota(jnp.int32, (tq, LANE), 1)
    qh, qshift, bound_max = [], [], None
    for s, half in heads:
        qf = jnp.where(qhi if half else qlo, q_ref[0, :, s * LANE:(s + 1) * LANE].astype(F32), 0.0)
        bound = jnp.sqrt(jnp.sum(qf * qf, axis=1, keepdims=True)) * kmax_scr[half][0:1, 0:1]
        qh.append(qf.astype(BF16))
        qshift.append(jnp.where(lane == (0 if half else HEAD_DIM), -bound, qf).astype(BF16))
        bound_max = jnp.max(bound) if bound_max is None else jnp.maximum(bound_max, jnp.max(bound))
    use_bound = 2.0 * bound_max < GQA_SHIFT_LIMIT

    def step_bound(k2, vt):
        sts = [_dot_nt(k2[half], qshift[idx]) for idx, (s, half) in enumerate(heads)]
        for idx, (s, half) in enumerate(heads):
            acc_scr[idx] = acc_scr[idx] + _dot(vt[half], jnp.exp2(sts[idx]).astype(BF16))

    def step_exact(k2, vt):
        sts = [_dot_nt(k2[half], qh[idx]) for idx, (s, half) in enumerate(heads)]
        for idx, (s, half) in enumerate(heads):
            m_old = m_scr[idx]
            m_new = jnp.maximum(m_old, jnp.max(sts[idx], axis=0, keepdims=True))
            pt = jnp.exp2(sts[idx] - m_new).astype(BF16)
            acc_scr[idx] = jnp.exp2(m_old - m_new) * acc_scr[idx] + _dot(vt[half], pt)
            m_scr[idx] = m_new

    def run(step):
        def body(c, carry):
            step(*key_chunk(c))
            return carry

        lax.fori_loop(0, n_chunks, body, 0)
        step([kc_ref[0, h] for h in range(2)], [vtc_ref[0, h] for h in range(2)])

    acc_scr[...] = jnp.zeros(acc_scr.shape, F32)

    @pl.when(use_bound)
    def _():
        run(step_bound)

    @pl.when(jnp.logical_not(use_bound))
    def _():
        m_scr[...] = jnp.full(m_scr.shape, -jnp.inf, F32)
        run(step_exact)

    row = lax.broadcasted_iota(jnp.int32, (LANE, tq), 0)
    for s in range(nslab):
        a_lo, a_hi = acc_scr[2 * s], acc_scr[2 * s + 1]
        out_t = jnp.where(row < HEAD_DIM, a_lo / a_lo[HEAD_DIM:HEAD_DIM + 1, :], a_hi / a_hi[0:1, :])
        o_ref[0, :, s * LANE:(s + 1) * LANE] = out_t.T.astype(BF16)


def _gqa(q, k_lat, vt_lat, k_ctx, vt_ctx, tq, tk):
    b, t, _ = q.shape
    n_lat = k_lat.shape[2]
    n_ctx = k_ctx.shape[2]
    full4 = lambda bi, i: (bi, 0, 0, 0)
    return pl.pallas_call(
        functools.partial(_gqa_kernel, tq, tk, n_lat),
        grid=(b, t // tq),
        in_specs=[
            pl.BlockSpec((1, tq, GQA_W), lambda bi, i: (bi, i, 0)),
            pl.BlockSpec((1, 2, n_lat, LANE), full4),
            pl.BlockSpec((1, 2, LANE, n_lat), full4),
            pl.BlockSpec((1, 2, n_ctx, LANE), full4),
            pl.BlockSpec((1, 2, LANE, n_ctx), full4),
        ],
        out_specs=pl.BlockSpec((1, tq, GQA_W), lambda bi, i: (bi, i, 0)),
        out_shape=jax.ShapeDtypeStruct((b, t, GQA_W), BF16),
        scratch_shapes=[
            pltpu.VMEM((GQA_KV_HEADS, 8, LANE), F32),
            pltpu.VMEM((GQA_HEADS, 1, tq), F32),
            pltpu.VMEM((GQA_HEADS, LANE, tq), F32),
        ],
        compiler_params=_cparams(("arbitrary", "arbitrary")),
        name="gqa",
    )(q, k_lat, vt_lat, k_ctx, vt_ctx)


def _ctxattn_kernel(nslab, shared_kv, q_ref, k_ref, v_ref, o_ref):
    n = q_ref.shape[1]
    lo, hi = _half_masks(n)
    ex = jnp.exp2 if shared_kv else jnp.exp
    for s in range(nslab):
        ks = 0 if shared_kv else s
        q = q_ref[0, :, s * LANE:(s + 1) * LANE]
        k = k_ref[0, :, ks * LANE:(ks + 1) * LANE]
        v = v_ref[0, :, ks * LANE:(ks + 1) * LANE]
        out = None
        for mask in (lo, hi):
            sc = _dot_nt(_sel(mask, q), k)
            p = ex(sc - jnp.max(sc, axis=1, keepdims=True))
            o_h = _dot(p.astype(BF16), _sel(mask, v)) / jnp.sum(p, axis=1, keepdims=True)
            out = o_h if out is None else out + o_h
        o_ref[0, :, s * LANE:(s + 1) * LANE] = out.astype(BF16)


def _ctxattn(q, k, v, shared_kv):
    b, n, w = q.shape
    kw = k.shape[2]
    full = lambda bi: (bi, 0, 0)
    return pl.pallas_call(
        functools.partial(_ctxattn_kernel, w // LANE, shared_kv),
        grid=(b,),
        in_specs=[pl.BlockSpec((1, n, w), full), pl.BlockSpec((1, n, kw), full), pl.BlockSpec((1, n, kw), full)],
        out_specs=pl.BlockSpec((1, n, w), full),
        out_shape=jax.ShapeDtypeStruct((b, n, w), BF16),
        compiler_params=_cparams(("arbitrary",)),
        name="ctxattn",
    )(q, k, v)


def _na_window_start(i, r, rows):
    return jnp.clip(i * r - NA_KH // 2, 0, rows - (r + NA_KH))


def _na_kernel(r, rows, q_ref, k_ref, v_ref, kc_ref, vc_ref, eb_ref, o_ref):
    i = pl.program_id(1)
    tq = r * GRID_W
    kwin = (r + NA_KH) * GRID_W
    start = pl.multiple_of(_na_window_start(i, r, rows) * GRID_W, GRID_W)
    qlo, qhi = _half_masks(tq)
    wlo, whi = _half_masks(kwin)
    clo, chi = _half_masks(kc_ref.shape[1])
    for s in range(NA_W // LANE):
        sl = slice(s * LANE, (s + 1) * LANE)
        q = q_ref[0, :, sl]
        kw = k_ref[0, pl.ds(start, kwin), sl]
        vw = v_ref[0, pl.ds(start, kwin), sl]
        kc = kc_ref[0, :, sl]
        vc = vc_ref[0, :, sl]
        out = None
        for half, (qm, wm, cm) in enumerate(((qlo, wlo, clo), (qhi, whi, chi))):
            qh = _sel(qm, q)
            s_loc = _dot_nt(qh, kw) + eb_ref[0, 2 * s + half]
            s_ctx = _dot_nt(qh, kc)
            m = jnp.maximum(jnp.max(s_loc, axis=1, keepdims=True), jnp.max(s_ctx, axis=1, keepdims=True))
            p_loc = jnp.exp(s_loc - m)
            p_ctx = jnp.exp(s_ctx - m)
            den = jnp.sum(p_loc, axis=1, keepdims=True) + jnp.sum(p_ctx, axis=1, keepdims=True)
            o_h = (_dot(p_loc.astype(BF16), _sel(wm, vw)) + _dot(p_ctx.astype(BF16), _sel(cm, vc))) / den
            out = o_h if out is None else out + o_h
        o_ref[0, :, sl] = out.astype(BF16)


def _na(q, k, v, k_ctx, v_ctx, eb, r):
    b, t, w = q.shape
    rows = t // GRID_W
    nblk = rows // r
    tq = r * GRID_W
    kwin = (r + NA_KH) * GRID_W
    n_ctx = k_ctx.shape[1]
    full = lambda bi, i: (bi, 0, 0)

    def variant(bi, i):
        return (jnp.where(i == 0, 0, jnp.where(i == nblk - 1, 2, 1)), 0, 0, 0)

    return pl.pallas_call(
        functools.partial(_na_kernel, r, rows),
        grid=(b, nblk),
        in_specs=[
            pl.BlockSpec((1, tq, w), lambda bi, i: (bi, i, 0)),
            pl.BlockSpec((1, t, w), full),
            pl.BlockSpec((1, t, w), full),
            pl.BlockSpec((1, n_ctx, w), full),
            pl.BlockSpec((1, n_ctx, w), full),
            pl.BlockSpec((1, NA_HEADS, tq, kwin), variant),
        ],
        out_specs=pl.BlockSpec((1, tq, w), lambda bi, i: (bi, i, 0)),
        out_shape=jax.ShapeDtypeStruct((b, t, w), BF16),
        compiler_params=_cparams(("arbitrary", "arbitrary")),
        name="na",
    )(q, k, v, k_ctx, v_ctx, eb)


def _na_bias_tables(rpb, r, rows):
    nblk = rows // r
    nkr = r + NA_KH
    qc = np.arange(GRID_W)
    kc = np.arange(GRID_W)
    cs = np.clip(qc - NA_KW // 2, 0, GRID_W - NA_KW)
    col_ok = (kc[None, :] >= cs[:, None]) & (kc[None, :] < cs[:, None] + NA_KW)
    dc = kc[None, :] - qc[:, None] + NA_KW - 1
    col_oh = (dc[None] == np.arange(2 * NA_KW - 1)[:, None, None]) & col_ok[None]
    t1 = jnp.einsum("hij,jqk->hiqk", rpb, jnp.asarray(col_oh.astype(np.float32)), precision=lax.Precision.HIGHEST)
    t1 = t1 + jnp.asarray(np.where(col_ok, 0.0, NEG).astype(np.float32))
    neg_blk = jnp.full((NA_HEADS, GRID_W, GRID_W), NEG, F32)
    variants = []
    for blk in (0, min(1, nblk - 1), nblk - 1):
        r0 = blk * r
        lo = int(np.clip(r0 - NA_KH // 2, 0, rows - nkr))
        rows_out = []
        for a in range(r):
            qr = r0 + a
            rs = int(np.clip(qr - NA_KH // 2, 0, rows - NA_KH))
            blks = []
            for bk in range(nkr):
                kr = lo + bk
                blks.append(t1[:, kr - qr + NA_KH - 1] if rs <= kr < rs + NA_KH else neg_blk)
            rows_out.append(jnp.concatenate(blks, axis=-1))
        variants.append(jnp.concatenate(rows_out, axis=1))
    return jnp.stack(variants)


def _layer_norm(x, g, b):
    xc = x - jnp.mean(x, axis=-1, keepdims=True)
    return xc * lax.rsqrt(jnp.mean(xc * xc, axis=-1, keepdims=True) + NORM_EPS) * g + b


def _post_kernel(hc, x_ref, of_ref, ob_ref, sg_ref, gq_ref, na_ref, mod_ref, hn_ref, bd_ref, wo_ref, w1_ref, w2_ref,
                 ln_ref, o_ref):
    x = x_ref[0]
    g1, sh2, sc2, g2 = (mod_ref[0, k:k + 1, :] for k in range(4))
    o = jnp.concatenate([of_ref[0, c] + ob_ref[0, c] for c in range(HG_W // LANE)], axis=1)
    ms = _dot((o * o).astype(BF16), bd_ref[...])
    hg = (o * lax.rsqrt(ms + NORM_EPS) * hn_ref[...] * sg_ref[0]).astype(BF16)
    y = (_dot(hg, wo_ref[0:HG_W, :]) + _dot(gq_ref[0], wo_ref[HG_W:HG_W + GQA_W, :])
         + _dot(na_ref[0], wo_ref[HG_W + GQA_W:, :]))
    x1 = _layer_norm(DN_ALPHA * x + g1 * y, ln_ref[0:1, :], ln_ref[1:2, :])
    h2 = (x1 * (1.0 + sc2) + sh2).astype(BF16)
    f = None
    for c in range(FFN_HIDDEN // hc):
        ug = _dot(h2, w1_ref[:, c * hc:(c + 1) * hc])
        uu = _dot(h2, w1_ref[:, FFN_HIDDEN + c * hc:FFN_HIDDEN + (c + 1) * hc])
        a = (ug * _sigmoid(ug) * uu).astype(BF16)
        part = _dot(a, w2_ref[c * hc:(c + 1) * hc, :])
        f = part if f is None else f + part
    o_ref[0] = _layer_norm(DN_ALPHA * x1 + g2 * f, ln_ref[2:3, :], ln_ref[3:4, :])


def _post(x, o_f, o_b, sg, gq, na, mod4, hn, bd256, w_out, w1, w2, ln4, tm, hc):
    b, t, d = x.shape
    row = lambda bi, i: (bi, i, 0)
    const2 = lambda bi, i: (0, 0)
    return pl.pallas_call(
        functools.partial(_post_kernel, hc),
        grid=(b, t // tm),
        in_specs=[
            pl.BlockSpec((1, tm, d), row),
            pl.BlockSpec((1, HG_W // LANE, tm, LANE), lambda bi, i: (bi, 0, i, 0)),
            pl.BlockSpec((1, HG_W // LANE, tm, LANE), lambda bi, i: (bi, 0, i, 0)),
            pl.BlockSpec((1, tm, HG_W), row),
            pl.BlockSpec((1, tm, GQA_W), row),
            pl.BlockSpec((1, tm, NA_W), row),
            pl.BlockSpec((1, 4, d), lambda bi, i: (bi, 0, 0)),
            pl.BlockSpec((1, HG_W), const2),
            pl.BlockSpec((HG_W, HG_W), const2),
            pl.BlockSpec((d, d), const2),
            pl.BlockSpec((d, 2 * FFN_HIDDEN), const2),
            pl.BlockSpec((FFN_HIDDEN, d), const2),
            pl.BlockSpec((4, d), const2),
        ],
        out_specs=pl.BlockSpec((1, tm, d), row),
        out_shape=jax.ShapeDtypeStruct((b, t, d), F32),
        compiler_params=_cparams(("arbitrary", "arbitrary")),
        name="post",
    )(x, o_f, o_b, sg, gq, na, mod4, hn, bd256, w_out, w1, w2, ln4)


_PERM_MAIN = np.concatenate([np.arange(0, HEAD_DIM, 2), np.arange(1, HEAD_DIM, 2)])
_PERM_PART = np.concatenate([np.arange(1, HEAD_DIM, 2), np.arange(0, HEAD_DIM, 2)])


def _w_in_ext(w_in):
    w = w_in.astype(BF16)
    gq0 = 5 * HG_W
    gk0 = gq0 + GQA_W
    gv0 = gk0 + GQA_KV_W

    def head(base, h, odd_first):
        pairs = w[:, :, base + h * HEAD_DIM:base + (h + 1) * HEAD_DIM].reshape(w.shape[0], w.shape[1], HEAD_DIM // 2, 2)
        even, odd = pairs[..., 0], pairs[..., 1]
        return jnp.concatenate([odd, even] if odd_first else [even, odd], axis=-1)

    cols = [w[:, :, :gq0]]
    cols += [head(gq0, h, False) for h in GQA_SLAB_HEADS] + [head(gk0, h, False) for h in range(GQA_KV_HEADS)]
    cols += [w[:, :, gv0:]]
    cols += [head(gq0, h, True) for h in GQA_SLAB_HEADS] + [head(gk0, h, True) for h in range(GQA_KV_HEADS)]
    return jnp.concatenate(cols, axis=-1)


def _w_out_rows(w_out):
    g0 = HG_W
    blocks = [w_out[:, :g0]] + [w_out[:, g0 + h * HEAD_DIM:g0 + (h + 1) * HEAD_DIM] for h in GQA_SLAB_HEADS]
    return jnp.concatenate(blocks + [w_out[:, g0 + GQA_W:]], axis=1).astype(BF16)


def _rope_tables(n_tokens):
    t = np.arange(n_tokens)
    row = (t // GRID_W).astype(np.float32)
    col = (t % GRID_W).astype(np.float32)
    n_pairs = HEAD_DIM // 4
    inv_freq = np.exp(-math.log(ROPE_THETA) * np.arange(n_pairs, dtype=np.float32) / n_pairs).astype(np.float32)
    ang = np.concatenate([row[:, None] * inv_freq, col[:, None] * inv_freq], -1).astype(np.float32)
    ang = jnp.asarray(ang)
    c, s = jnp.cos(ang), jnp.sin(ang)
    cos = jnp.tile(jnp.concatenate([c, c], -1), (1, LANE // HEAD_DIM))
    sin = jnp.tile(jnp.concatenate([-s, s], -1), (1, LANE // HEAD_DIM))
    return cos, sin


def _block_diag_mean(width):
    c = np.arange(width)
    return jnp.asarray(((c[:, None] // HEAD_DIM) == (c[None, :] // HEAD_DIM)).astype(np.float32) / HEAD_DIM, BF16)


def kernel(x, c, ctx, c_ctx, w_ada, b_ada, w_in, hgrn_lb_logits, hgrn_norm, gqa_q_norm, gqa_k_norm, na_rpb, w_out,
           w_ffn_in, w_ffn_out, ln_g, ln_b):
    b, n, d = x.shape
    n_ctx = ctx.shape[1]
    rows = n // GRID_W
    tm = min(512, n)
    tm_ctx = min(256, n_ctx)
    tb = 256
    tq, tk = min(256, n), min(1024, n)
    na_r = 4
    hc = FFN_HIDDEN // 2

    cv = jnp.zeros((MOD_ROWS, d), F32).at[:b].set(c).at[b].set(c_ctx)
    mods = _mods(cv, w_ada, b_ada)

    w_ext = _w_in_ext(w_in)
    w_out_p = _w_out_rows(w_out)
    w1 = w_ffn_in.astype(BF16)
    w2 = w_ffn_out.astype(BF16)

    cos_l, sin_l = _rope_tables(n)
    cos_c = jnp.ones((n_ctx, LANE), F32)
    sin_c = jnp.zeros((n_ctx, LANE), F32)
    bd128 = _block_diag_mean(LANE)
    bd256 = _block_diag_mean(HG_W)
    hg_consts = _hgrn_consts()
    s_zero = jnp.zeros((b, 2, HG_W, HG_W), F32)

    x_lat, x_ctx = x, ctx
    for l in range(DEPTH):
        need_ctx = l < DEPTH - 1
        m = mods[l].reshape(MOD_ROWS, N_MOD, d)
        m_lat = m[:b]
        m_ctx = jnp.broadcast_to(m[b][None], (b, N_MOD, d))
        qscale = HEAD_DIM ** -0.5 * math.log2(math.e)
        gains = (
            (jnp.tile(gqa_q_norm[l][_PERM_MAIN], GQA_HEADS) * qscale)[None],
            (jnp.tile(gqa_q_norm[l][_PERM_PART], GQA_HEADS) * qscale)[None],
            jnp.tile(gqa_k_norm[l][_PERM_MAIN], GQA_KV_HEADS)[None],
            jnp.tile(gqa_k_norm[l][_PERM_PART], GQA_KV_HEADS)[None],
        )
        p_lat = _inproj(l, x_lat, m_lat[:, 0:1], m_lat[:, 1:2], w_ext[l], hgrn_lb_logits, cos_l, sin_l, gains, bd128, tm)
        p_ctx = _inproj(l, x_ctx, m_ctx[:, 0:1], m_ctx[:, 1:2], w_ext[l], hgrn_lb_logits, cos_c, sin_c, gains, bd128,
                        tm_ctx)
        hq_l, lf_l, hi_l, sg_l, gq_l, gk_l, gv_l, nq_l, nk_l, nv_l, gvt_l, gk2_l = p_lat
        hq_c, lf_c, hi_c, sg_c, gq_c, gk_c, gv_c, nq_c, nk_c, nv_c, gvt_c, gk2_c = p_ctx

        *o_c, s_c = _hgrn(hq_c, lf_c, hi_c, s_zero, hg_consts, min(tb, n_ctx))
        *o_l, _ = _hgrn(hq_l, lf_l, hi_l, s_c, hg_consts, tb)

        gqa_l = _gqa(gq_l, gk2_l, gvt_l, gk2_c, gvt_c, tq, tk)
        eb = _na_bias_tables(na_rpb[l], na_r, rows)
        na_l = _na(nq_l, nk_l, nv_l, nk_c, nv_c, eb, na_r)

        hn = jnp.tile(hgrn_norm[l], HG_HEADS)[None]
        ln4 = jnp.stack([ln_g[l, 0], ln_b[l, 0], ln_g[l, 1], ln_b[l, 1]])
        if need_ctx:
            gqa_c = _ctxattn(gq_c, gk_c, gv_c, True)
            na_c = _ctxattn(nq_c, nk_c, nv_c, False)
            x_ctx = _post(x_ctx, o_c[0], o_c[1], sg_c, gqa_c, na_c, m_ctx[:, 2:6], hn, bd256, w_out_p[l], w1[l], w2[l],
                          ln4, tm_ctx, hc)
        x_lat = _post(x_lat, o_l[0], o_l[1], sg_l, gqa_l, na_l, m_lat[:, 2:6], hn, bd256, w_out_p[l], w1[l], w2[l],
                      ln4, tm, hc)
    return x_lat
```

```python
import functools
import math

import numpy as np
import jax
import jax.numpy as jnp
from jax import lax
from jax.experimental import pallas as pl
from jax.experimental.pallas import tpu as pltpu

F32 = jnp.float32
BF16 = jnp.bfloat16

D_MODEL = 1024
DEPTH = 2
GRID_W = 64
HEAD_DIM = 64
HG_HEADS = 4
GQA_HEADS = 6
GQA_KV_HEADS = 2
NA_HEADS = 6
NA_KH = 8
NA_KW = 16
ROPE_THETA = 10000.0
NORM_EPS = 1e-6
N_MOD = 6
HG_W = HG_HEADS * HEAD_DIM
GQA_W = GQA_HEADS * HEAD_DIM
GQA_KV_W = GQA_KV_HEADS * HEAD_DIM
NA_W = NA_HEADS * HEAD_DIM
FFN_HIDDEN = 2816
DN_ALPHA = (2 * DEPTH) ** 0.25

LANE = 128
SUB = 16
MOD_ROWS = 8
NEG = -1e30
GQA_SHIFT_LIMIT = 100.0
VMEM_LIMIT = 56 * 1024 * 1024

C_HQ, C_FF, C_FB, C_HI, C_HGATE = 0, 256, 512, 768, 1024
C_GQ, C_GK, C_GV = 1280, 1664, 1792
C_NQ, C_NK, C_NV = 1920, 2304, 2688
C_GQP, C_GKP = 3072, 3456
IN_EXT = 3584
GQA_SLAB_HEADS = (0, 3, 1, 4, 2, 5)


def _cparams(sem, flags=None):
    return pltpu.CompilerParams(dimension_semantics=sem, vmem_limit_bytes=VMEM_LIMIT, flags=flags)


def _sigmoid(z):
    return 1.0 / (1.0 + jnp.exp(-z))


def _dot(a, b):
    return jnp.dot(a, b, preferred_element_type=F32)


def _dot_nt(a, b):
    return lax.dot_general(a, b, (((1,), (1,)), ((), ())), preferred_element_type=F32)


def _mods_kernel(cv_ref, w_ref, b_ref, o_ref):
    cv = cv_ref[...]
    a = cv * _sigmoid(cv)
    o_ref[0] = jnp.dot(a, w_ref[0], preferred_element_type=F32, precision=lax.Precision.HIGHEST) + b_ref[0]


def _mods(cv, w_ada, b_ada):
    depth, d, nm = w_ada.shape
    tn = 1024
    return pl.pallas_call(
        _mods_kernel,
        grid=(depth, nm // tn),
        in_specs=[
            pl.BlockSpec((MOD_ROWS, d), lambda l, j: (0, 0)),
            pl.BlockSpec((1, d, tn), lambda l, j: (l, 0, j)),
            pl.BlockSpec((1, 1, tn), lambda l, j: (l, 0, j)),
        ],
        out_specs=pl.BlockSpec((1, MOD_ROWS, tn), lambda l, j: (l, 0, j)),
        out_shape=jax.ShapeDtypeStruct((depth, MOD_ROWS, nm), F32),
        compiler_params=_cparams(("arbitrary", "arbitrary")),
        name="mods",
    )(cv, w_ada, b_ada.reshape(depth, 1, nm))


def _inproj_kernel(layer, x_ref, sh_ref, sc_ref, w_ref, lbl_ref, cos_ref, sin_ref, gq_ref, gqp_ref, gk_ref, gkp_ref,
                   bd_ref, hq_ref, lf_ref, hi_ref, hg_ref, gq_o, gk_o, gv_o, nq_o, nk_o, nv_o, gvt_o, gk2_o):
    x = x_ref[0]
    h = (x * (1.0 + sc_ref[0]) + sh_ref[0]).astype(BF16)

    def proj(c0, width):
        return _dot(h, w_ref[:, c0:c0 + width])

    logits = [lbl_ref[d] for d in range(DEPTH)]
    mx = functools.reduce(jnp.maximum, logits)
    es = [jnp.exp(v - mx) for v in logits]
    inv = 1.0 / functools.reduce(lambda a, b: a + b, es)
    ps = [e * inv for e in es]
    lb = jnp.clip(functools.reduce(lambda a, b: a + b, ps[:layer + 1]) - ps[0], 0.0, 1.0)

    def put_halves(ref, first, a):
        for c in range(HG_W // LANE):
            ref[0, first + c] = a[:, c * LANE:(c + 1) * LANE]

    q = proj(C_HQ, HG_W)
    put_halves(hq_ref, 0, q * _sigmoid(q))
    for d, c0 in enumerate((C_FF, C_FB)):
        z = proj(c0, HG_W)
        lbd = lb[d:d + 1, :]
        put_halves(lf_ref, d * (HG_W // LANE), jnp.log(lbd + (1.0 - lbd) * _sigmoid(z)))
    put_halves(hi_ref, 0, proj(C_HI, HG_W))
    g = proj(C_HGATE, HG_W)
    hg_ref[0] = g * _sigmoid(g)

    cos = cos_ref[...]
    sin = sin_ref[...]
    bd = bd_ref[...]

    rot_w = GQA_W + GQA_KV_W
    main = proj(C_GQ, rot_w)
    part = proj(C_GQP, rot_w)
    nq_slabs = GQA_W // LANE
    for s in range(rot_w // LANE):
        sl = slice(s * LANE, (s + 1) * LANE)
        pm, pp = main[:, sl], part[:, sl]
        ms = _dot((pm * pm).astype(BF16), bd)
        r = lax.rsqrt(ms + NORM_EPS)
        if s < nq_slabs:
            gm, gp, out_ref, osl = gq_ref[:, sl], gqp_ref[:, sl], gq_o, sl
        else:
            osl = slice((s - nq_slabs) * LANE, (s - nq_slabs + 1) * LANE)
            gm, gp, out_ref = gk_ref[:, osl], gkp_ref[:, osl], gk_o
        out_ref[0, :, osl] = ((pm * r * gm) * cos + (pp * r * gp) * sin).astype(BF16)

    gv_nq = proj(C_GV, GQA_KV_W + NA_W)
    gv = gv_nq[:, :GQA_KV_W]
    gv_o[0] = gv.astype(BF16)
    nq_o[0] = (gv_nq[:, GQA_KV_W:] * (HEAD_DIM ** -0.5 * math.log2(math.e))).astype(BF16)
    nk_nv = proj(C_NK, 2 * NA_W)
    nk_o[0] = nk_nv[:, :NA_W].astype(BF16)
    nv_o[0] = nk_nv[:, NA_W:].astype(BF16)
    lane = lax.broadcasted_iota(jnp.int32, gv.shape, 1)
    gvt_o[0, 0] = jnp.where(lane < HEAD_DIM, gv, jnp.where(lane == HEAD_DIM, 1.0, 0.0)).T.astype(BF16)
    gvt_o[0, 1] = jnp.where(lane >= HEAD_DIM, gv, jnp.where(lane == 0, 1.0, 0.0)).T.astype(BF16)
    gk = gk_o[0].astype(F32)
    gk2_o[0, 0] = jnp.where(lane < HEAD_DIM, gk, jnp.where(lane == HEAD_DIM, 1.0, 0.0)).astype(BF16)
    gk2_o[0, 1] = jnp.where(lane >= HEAD_DIM, gk, jnp.where(lane == 0, 1.0, 0.0)).astype(BF16)


def _inproj(layer, x, shift, scale, w_ext, lb_logits, cos, sin, gains, bd128, tm):
    b, t, d = x.shape
    gq, gqp, gk, gkp = gains
    row = lambda bi, i: (bi, i, 0)
    const2 = lambda bi, i: (0, 0)
    halves = (HG_W // LANE, 2 * HG_W // LANE, HG_W // LANE)
    widths = (HG_W, GQA_W, GQA_KV_W, GQA_KV_W, NA_W, NA_W, NA_W)
    dtypes = (F32, BF16, BF16, BF16, BF16, BF16, BF16)
    return pl.pallas_call(
        functools.partial(_inproj_kernel, layer),
        grid=(b, t // tm),
        in_specs=[
            pl.BlockSpec((1, tm, d), row),
            pl.BlockSpec((1, 1, d), lambda bi, i: (bi, 0, 0)),
            pl.BlockSpec((1, 1, d), lambda bi, i: (bi, 0, 0)),
            pl.BlockSpec((d, IN_EXT), const2),
            pl.BlockSpec((DEPTH, 2, HG_W), lambda bi, i: (0, 0, 0)),
            pl.BlockSpec((tm, LANE), lambda bi, i: (i, 0)),
            pl.BlockSpec((tm, LANE), lambda bi, i: (i, 0)),
            pl.BlockSpec((1, GQA_W), const2),
            pl.BlockSpec((1, GQA_W), const2),
            pl.BlockSpec((1, GQA_KV_W), const2),
            pl.BlockSpec((1, GQA_KV_W), const2),
            pl.BlockSpec((LANE, LANE), const2),
        ],
        out_specs=[pl.BlockSpec((1, nh, tm, LANE), lambda bi, i: (bi, 0, i, 0)) for nh in halves]
        + [pl.BlockSpec((1, tm, w), row) for w in widths]
        + [pl.BlockSpec((1, 2, LANE, tm), lambda bi, i: (bi, 0, 0, i)),
           pl.BlockSpec((1, 2, tm, LANE), lambda bi, i: (bi, 0, i, 0))],
        out_shape=[jax.ShapeDtypeStruct((b, nh, t, LANE), F32) for nh in halves]
        + [jax.ShapeDtypeStruct((b, t, w), dt) for w, dt in zip(widths, dtypes)]
        + [jax.ShapeDtypeStruct((b, 2, LANE, t), BF16), jax.ShapeDtypeStruct((b, 2, t, LANE), BF16)],
        compiler_params=_cparams(("arbitrary", "arbitrary")),
        name="inproj",
    )(x, shift, scale, w_ext, lb_logits, cos, sin, gq, gqp, gk, gkp, bd128)


def _hgrn_kernel(tb, qf_ref, lff_ref, vf_ref, qb_ref, lfb_ref, vb_ref, s0_ref, ind_ref, bm_ref,
                 of_ref, ob_ref, sfin_ref, st_scr, stb_scr, qs_scr, ks_scr):
    i = pl.program_id(1)
    nsc = tb // SUB

    @pl.when(i == 0)
    def _():
        st_scr[...] = s0_ref[0]
        stb_scr[...] = s0_ref[0].astype(BF16)

    ind = ind_ref[...]
    in_refs = ((qf_ref, lff_ref, vf_ref), (qb_ref, lfb_ref, vb_ref))
    o_refs = (of_ref, ob_ref)

    nh = HG_W // LANE

    def slab(tt):
        return pl.ds(tt, nsc, stride=SUB)

    def load(ref, lead, rows):
        return jnp.concatenate([ref[(*lead, c, rows, slice(None))] for c in range(nh)], axis=1)

    def store(ref, lead, rows, a, add=False):
        for c in range(nh):
            idx = (*lead, c, rows, slice(None))
            piece = a[:, c * LANE:(c + 1) * LANE]
            ref[idx] = ref[idx] + piece if add else piece

    orders = (list(range(SUB)), list(range(SUB - 1, -1, -1)))
    q, v, f, k, dec = [], [], [], [], []
    for d, (q_ref, lf_ref, v_ref) in enumerate(in_refs):
        lf = {tt: load(lf_ref, (0,), slab(tt)) for tt in orders[d]}
        q.append({tt: load(q_ref, (0,), slab(tt)) for tt in orders[d]})
        v.append({tt: load(v_ref, (0,), slab(tt)) for tt in orders[d]})
        f.append({tt: jnp.exp(lf[tt]) for tt in orders[d]})
        k.append({tt: 1.0 - f[d][tt] for tt in orders[d]})
        cum, run = {}, None
        for tt in orders[d]:
            run = lf[tt] if run is None else run + lf[tt]
            cum[tt] = run
        for tt in orders[d]:
            store(qs_scr, (d,), slab(tt), q[d][tt] * jnp.exp(cum[tt]))
            store(ks_scr, (d,), slab(tt), k[d][tt] * jnp.exp(run - cum[tt]))
        dec.append(jnp.exp(run))

    acc = [dict(), dict()]
    for a in range(SUB):
        for d in range(2):
            ss = orders[d][a]
            u = k[d][ss]
            es = []
            for tt in orders[d][a:]:
                if tt != ss:
                    u = u * f[d][tt]
                es.append((q[d][tt] * u).astype(BF16))
            att = _dot(jnp.concatenate(es, axis=0), ind)
            for n, tt in enumerate(orders[d][a:]):
                part = att[n * nsc:(n + 1) * nsc] * v[d][ss]
                acc[d][tt] = part if a == 0 else acc[d][tt] + part
    for d in range(2):
        for tt in orders[d]:
            store(o_refs[d], (0,), slab(tt), acc[d][tt])

    bm = bm_ref[...]
    for step in range(nsc):
        for d in range(2):
            j = nsc - 1 - step if d else step
            sl = slice(j * SUB, (j + 1) * SUB)
            store(o_refs[d], (0,), sl, _dot_nt(load(qs_scr, (d,), sl).astype(BF16), stb_scr[d]), add=True)
            kvt = lax.dot_general(load(in_refs[d][2], (0,), sl).astype(BF16), load(ks_scr, (d,), sl).astype(BF16),
                                  (((0,), (0,)), ((), ())), preferred_element_type=F32)
            for h in range(HG_HEADS):
                rs = slice(h * HEAD_DIM, (h + 1) * HEAD_DIM)
                ls = slice((h // 2) * LANE, (h // 2 + 1) * LANE)
                blk = st_scr[d, rs, ls] * dec[d][j:j + 1, ls] + kvt[rs, ls] * bm[rs, ls]
                st_scr[d, rs, ls] = blk
                stb_scr[d, rs, ls] = blk.astype(BF16)

    @pl.when(i == pl.num_programs(1) - 1)
    def _():
        sfin_ref[0] = st_scr[...]


def _hgrn(q, lf2, v, s0, consts, tb):
    b, nh, t, _ = q.shape
    nblk = t // tb
    ind, bm = consts
    fwd = lambda bi, i: (bi, 0, i, 0)
    bwd = lambda bi, i: (bi, 0, nblk - 1 - i, 0)
    bwd_lf = lambda bi, i: (bi, 1, nblk - 1 - i, 0)
    const2 = lambda bi, i: (0, 0)
    state = lambda bi, i: (bi, 0, 0, 0)
    blk = (1, nh, tb, LANE)
    return pl.pallas_call(
        functools.partial(_hgrn_kernel, tb),
        grid=(b, nblk),
        in_specs=[
            pl.BlockSpec(blk, fwd), pl.BlockSpec(blk, fwd), pl.BlockSpec(blk, fwd),
            pl.BlockSpec(blk, bwd), pl.BlockSpec(blk, bwd_lf), pl.BlockSpec(blk, bwd),
            pl.BlockSpec((1, 2, HG_W, HG_W), state),
            pl.BlockSpec((HG_W, HG_W), const2),
            pl.BlockSpec((HG_W, HG_W), const2),
        ],
        out_specs=[pl.BlockSpec(blk, fwd), pl.BlockSpec(blk, bwd), pl.BlockSpec((1, 2, HG_W, HG_W), state)],
        out_shape=[
            jax.ShapeDtypeStruct((b, nh, t, LANE), F32),
            jax.ShapeDtypeStruct((b, nh, t, LANE), F32),
            jax.ShapeDtypeStruct((b, 2, HG_W, HG_W), F32),
        ],
        scratch_shapes=[pltpu.VMEM((2, HG_W, HG_W), F32), pltpu.VMEM((2, HG_W, HG_W), BF16),
                        pltpu.VMEM((2, nh, tb, LANE), F32), pltpu.VMEM((2, nh, tb, LANE), F32)],
        compiler_params=_cparams(("arbitrary", "arbitrary")),
        name="hgrn",
    )(q, lf2, v, q, lf2, v, s0, ind, bm)


def _hgrn_consts():
    c = np.arange(HG_W)
    blk = (c[:, None] // HEAD_DIM) == (c[None, :] // HEAD_DIM)
    return jnp.asarray(blk.astype(np.float32), BF16), jnp.asarray(blk.astype(np.float32))


def _half_masks(rows):
    lane = lax.broadcasted_iota(jnp.int32, (rows, LANE), 1)
    return lane < HEAD_DIM, lane >= HEAD_DIM


def _sel(mask, a):
    return jnp.where(mask, a, jnp.zeros_like(a))


def _value_slabs(v):
    lane = lax.broadcasted_iota(jnp.int32, v.shape, 1)
    lo, hi = _half_masks(v.shape[0])
    e_lo = jnp.where(lane == HEAD_DIM, 1.0, 0.0).astype(v.dtype)
    e_hi = jnp.where(lane == 0, 1.0, 0.0).astype(v.dtype)
    return _sel(lo, v) + e_lo, _sel(hi, v) + e_hi


def _normalize_slab(a_lo, a_hi):
    lane = lax.broadcasted_iota(jnp.int32, a_lo.shape, 1)
    return jnp.where(lane < HEAD_DIM, a_lo / a_lo[:, HEAD_DIM:HEAD_DIM + 1], a_hi / a_hi[:, 0:1])


def _gqa_kernel(tq, tk, n_lat, q_ref, kl_ref, vtl_ref, kc_ref, vtc_ref, o_ref, kmax_scr, m_scr, acc_scr):
    i = pl.program_id(1)
    nslab = GQA_W // LANE
    n_chunks = n_lat // tk
    heads = [(s, half) for s in range(nslab) for half in range(2)]

    def key_chunk(c):
        start = pl.multiple_of(c * tk, tk)
        return ([kl_ref[0, h, pl.ds(start, tk), :] for h in range(2)],
                [vtl_ref[0, h, :, pl.ds(start, tk)] for h in range(2)])

    @pl.when(i == 0)
    def _():
        for h in range(2):
            def norm2(kb):
                x = kb.astype(F32)
                lane = lax.broadcasted_iota(jnp.int32, x.shape, 1)
                x = jnp.where((lane < HEAD_DIM) if h == 0 else (lane >= HEAD_DIM), x, 0.0)
                mx = jnp.max(jnp.sum(x * x, axis=1, keepdims=True), axis=0, keepdims=True)
                return jnp.broadcast_to(mx, kmax_scr.shape[1:])

            mx = lax.fori_loop(0, n_chunks, lambda c, m: jnp.maximum(m, norm2(key_chunk(c)[0][h])),
                               norm2(kc_ref[0, h]))
            kmax_scr[h] = jnp.sqrt(mx)

    qlo, qhi = _half_masks(tq)
    lane = lax.broadcasted_iota(jnp.int32, (tq, LANE), 1)
    q_all = q_ref[0].astype(F32)
    rr = lax.broadcasted_iota(jnp.int32, (GQA_W, GQA_W), 0) // HEAD_DIM
    cc = lax.broadcasted_iota(jnp.int32, (GQA_W, GQA_W), 1) // HEAD_DIM
    qn2 = _dot((q_all * q_all).astype(BF16), jnp.where(rr == cc, 1.0, 0.0).astype(BF16))
    kmax = jnp.where(lane[0:1] < HEAD_DIM, kmax_scr[0][0:1], kmax_scr[1][0:1])
    bound = jnp.sqrt(qn2) * jnp.concatenate([kmax] * nslab, axis=1)
    use_bound = 2.0 * jnp.max(bound) < GQA_SHIFT_LIMIT

    def query(idx, bound_path):
        s, half = heads[idx]
        qf = q_all[:, s * LANE:(s + 1) * LANE]
        if not bound_path:
            return jnp.where(qhi if half else qlo, qf, 0.0).astype(BF16)
        swapped = pltpu.roll(bound[:, s * LANE:(s + 1) * LANE], HEAD_DIM, 1)
        return jnp.where(lane == (0 if half else HEAD_DIM), -swapped, qf).astype(BF16)

    qshift = [query(idx, True) for idx in range(len(heads))]

    def step_bound(k2, vt):
        sts = [_dot_nt(k2[half], qshift[idx]) for idx, (s, half) in enumerate(heads)]
        for idx, (s, half) in enumerate(heads):
            acc_scr[idx] = acc_scr[idx] + _dot(vt[half], jnp.exp2(sts[idx]).astype(BF16))

    def step_exact(qh, k2, vt):
        sts = [_dot_nt(k2[half], qh[idx]) for idx, (s, half) in enumerate(heads)]
        for idx, (s, half) in enumerate(heads):
            m_old = m_scr[idx]
            m_new = jnp.maximum(m_old, jnp.max(sts[idx], axis=0, keepdims=True))
            pt = jnp.exp2(sts[idx] - m_new).astype(BF16)
            acc_scr[idx] = jnp.exp2(m_old - m_new) * acc_scr[idx] + _dot(vt[half], pt)
            m_scr[idx] = m_new

    def run(step):
        def body(c, carry):
            step(*key_chunk(c))
            return carry

        lax.fori_loop(0, n_chunks, body, 0)
        step([kc_ref[0, h] for h in range(2)], [vtc_ref[0, h] for h in range(2)])

    acc_scr[...] = jnp.zeros(acc_scr.shape, F32)

    @pl.when(use_bound)
    def _():
        run(step_bound)

    @pl.when(jnp.logical_not(use_bound))
    def _():
        m_scr[...] = jnp.full(m_scr.shape, -jnp.inf, F32)
        run(functools.partial(step_exact, [query(idx, False) for idx in range(len(heads))]))

    row = lax.broadcasted_iota(jnp.int32, (LANE, tq), 0)
    for s in range(nslab):
        a_lo, a_hi = acc_scr[2 * s], acc_scr[2 * s + 1]
        out_t = jnp.where(row < HEAD_DIM, a_lo / a_lo[HEAD_DIM:HEAD_DIM + 1, :], a_hi / a_hi[0:1, :])
        o_ref[0, :, s * LANE:(s + 1) * LANE] = out_t.T.astype(BF16)


def _gqa(q, k_lat, vt_lat, k_ctx, vt_ctx, tq, tk):
    b, t, _ = q.shape
    n_lat = k_lat.shape[2]
    n_ctx = k_ctx.shape[2]
    full4 = lambda bi, i: (bi, 0, 0, 0)
    return pl.pallas_call(
        functools.partial(_gqa_kernel, tq, tk, n_lat),
        grid=(b, t // tq),
        in_specs=[
            pl.BlockSpec((1, tq, GQA_W), lambda bi, i: (bi, i, 0)),
            pl.BlockSpec((1, 2, n_lat, LANE), full4),
            pl.BlockSpec((1, 2, LANE, n_lat), full4),
            pl.BlockSpec((1, 2, n_ctx, LANE), full4),
            pl.BlockSpec((1, 2, LANE, n_ctx), full4),
        ],
        out_specs=pl.BlockSpec((1, tq, GQA_W), lambda bi, i: (bi, i, 0)),
        out_shape=jax.ShapeDtypeStruct((b, t, GQA_W), BF16),
        scratch_shapes=[
            pltpu.VMEM((GQA_KV_HEADS, 8, LANE), F32),
            pltpu.VMEM((GQA_HEADS, 1, tq), F32),
            pltpu.VMEM((GQA_HEADS, LANE, tq), F32),
        ],
        compiler_params=_cparams(("arbitrary", "arbitrary")),
        name="gqa",
    )(q, k_lat, vt_lat, k_ctx, vt_ctx)


def _ctxattn_kernel(nslab, shared_kv, q_ref, k_ref, v_ref, o_ref):
    n = q_ref.shape[1]
    lo, hi = _half_masks(n)
    for s in range(nslab):
        ks = 0 if shared_kv else s
        q = q_ref[0, :, s * LANE:(s + 1) * LANE]
        k = k_ref[0, :, ks * LANE:(ks + 1) * LANE]
        v = v_ref[0, :, ks * LANE:(ks + 1) * LANE]
        out = None
        for mask in (lo, hi):
            sc = _dot_nt(_sel(mask, q), k)
            p = jnp.exp2(sc - jnp.max(sc, axis=1, keepdims=True))
            o_h = _dot(p.astype(BF16), _sel(mask, v)) / jnp.sum(p, axis=1, keepdims=True)
            out = o_h if out is None else out + o_h
        o_ref[0, :, s * LANE:(s + 1) * LANE] = out.astype(BF16)


def _ctxattn(q, k, v, shared_kv):
    b, n, w = q.shape
    kw = k.shape[2]
    full = lambda bi: (bi, 0, 0)
    return pl.pallas_call(
        functools.partial(_ctxattn_kernel, w // LANE, shared_kv),
        grid=(b,),
        in_specs=[pl.BlockSpec((1, n, w), full), pl.BlockSpec((1, n, kw), full), pl.BlockSpec((1, n, kw), full)],
        out_specs=pl.BlockSpec((1, n, w), full),
        out_shape=jax.ShapeDtypeStruct((b, n, w), BF16),
        compiler_params=_cparams(("arbitrary",)),
        name="ctxattn",
    )(q, k, v)


def _na_window_start(i, r, rows):
    return jnp.clip(i * r - NA_KH // 2, 0, rows - (r + NA_KH))


def _na_kernel(r, rows, q_ref, k_ref, v_ref, kc_ref, vc_ref, eb_ref, br_ref, o_ref, kmax_scr):
    i = pl.program_id(1)
    tq = r * GRID_W
    kwin = (r + NA_KH) * GRID_W
    n_lat = k_ref.shape[1]
    n_ctx = kc_ref.shape[1]
    heads = [(s, half) for s in range(NA_W // LANE) for half in range(2)]

    @pl.when(i == 0)
    def _():
        rr = lax.broadcasted_iota(jnp.int32, (NA_W, NA_W), 0) // HEAD_DIM
        cc = lax.broadcasted_iota(jnp.int32, (NA_W, NA_W), 1) // HEAD_DIM
        ones_bd = jnp.where(rr == cc, 1.0, 0.0).astype(BF16)

        def norm2(kb):
            x = kb.astype(F32)
            n2 = _dot((x * x).astype(BF16), ones_bd)
            return jnp.broadcast_to(jnp.max(n2, axis=0, keepdims=True), kmax_scr.shape)

        ck = min(1024, n_lat)
        mx = lax.fori_loop(0, n_lat // ck,
                           lambda c, m: jnp.maximum(m, norm2(k_ref[0, pl.ds(pl.multiple_of(c * ck, ck), ck), :])),
                           norm2(kc_ref[0]))
        kmax_scr[...] = jnp.sqrt(mx)

    start = pl.multiple_of(_na_window_start(i, r, rows) * GRID_W, GRID_W)
    lane = lax.broadcasted_iota(jnp.int32, (tq, LANE), 1)
    qlo, qhi = _half_masks(tq)

    def ones_lane(n, half, dtype):
        ln = lax.broadcasted_iota(jnp.int32, (n, LANE), 1)
        return jnp.where(ln == (0 if half else HEAD_DIM), 1.0, 0.0).astype(dtype)

    q_all = q_ref[0].astype(F32)
    rr = lax.broadcasted_iota(jnp.int32, (NA_W, NA_W), 0) // HEAD_DIM
    cc = lax.broadcasted_iota(jnp.int32, (NA_W, NA_W), 1) // HEAD_DIM
    qn2 = _dot((q_all * q_all).astype(BF16), jnp.where(rr == cc, 1.0, 0.0).astype(BF16))
    reach = jnp.sqrt(qn2) * kmax_scr[0:1, :]
    shift = reach + br_ref[0][0:1, :]
    use_bound = jnp.max(2.0 * reach + (br_ref[0][0:1, :] - br_ref[1][0:1, :])) < GQA_SHIFT_LIMIT
    def query(h, bound_path):
        s, half = heads[h]
        qf = q_all[:, s * LANE:(s + 1) * LANE]
        if not bound_path:
            return jnp.where(qhi if half else qlo, qf, 0.0).astype(BF16)
        swapped = pltpu.roll(shift[:, s * LANE:(s + 1) * LANE], HEAD_DIM, 1)
        return jnp.where(lane == (0 if half else HEAD_DIM), -swapped, qf).astype(BF16)

    def attend(bound_path):
        for s in range(NA_W // LANE):
            sl = slice(s * LANE, (s + 1) * LANE)
            kw = k_ref[0, pl.ds(start, kwin), sl]
            kc = kc_ref[0, :, sl]
            vws = _value_slabs(v_ref[0, pl.ds(start, kwin), sl])
            vcs = _value_slabs(vc_ref[0, :, sl])
            wm, cm = _half_masks(kwin), _half_masks(n_ctx)
            outs = []
            for half in range(2):
                h = 2 * s + half
                qv = query(h, bound_path)
                if bound_path:
                    s_loc = _dot_nt(qv, _sel(wm[half], kw) + ones_lane(kwin, half, BF16)) + eb_ref[0, h]
                    s_ctx = _dot_nt(qv, _sel(cm[half], kc) + ones_lane(n_ctx, half, BF16))
                else:
                    s_loc = _dot_nt(qv, kw) + eb_ref[0, h]
                    s_ctx = _dot_nt(qv, kc)
                    m = jnp.maximum(jnp.max(s_loc, axis=1, keepdims=True), jnp.max(s_ctx, axis=1, keepdims=True))
                    s_loc, s_ctx = s_loc - m, s_ctx - m
                outs.append(_dot(jnp.exp2(s_loc).astype(BF16), vws[half])
                            + _dot(jnp.exp2(s_ctx).astype(BF16), vcs[half]))
            o_ref[0, :, sl] = _normalize_slab(outs[0], outs[1]).astype(BF16)

    @pl.when(use_bound)
    def _():
        attend(True)

    @pl.when(jnp.logical_not(use_bound))
    def _():
        attend(False)


def _na(q, k, v, k_ctx, v_ctx, eb, brange, r):
    b, t, w = q.shape
    rows = t // GRID_W
    nblk = rows // r
    tq = r * GRID_W
    kwin = (r + NA_KH) * GRID_W
    n_ctx = k_ctx.shape[1]
    full = lambda bi, i: (bi, 0, 0)

    def variant(bi, i):
        return (jnp.where(i == 0, 0, jnp.where(i == nblk - 1, 2, 1)), 0, 0, 0)

    return pl.pallas_call(
        functools.partial(_na_kernel, r, rows),
        grid=(b, nblk),
        in_specs=[
            pl.BlockSpec((1, tq, w), lambda bi, i: (bi, i, 0)),
            pl.BlockSpec((1, t, w), full),
            pl.BlockSpec((1, t, w), full),
            pl.BlockSpec((1, n_ctx, w), full),
            pl.BlockSpec((1, n_ctx, w), full),
            pl.BlockSpec((1, NA_HEADS, tq, kwin), variant),
            pl.BlockSpec((2, 8, NA_W), lambda bi, i: (0, 0, 0)),
        ],
        out_specs=pl.BlockSpec((1, tq, w), lambda bi, i: (bi, i, 0)),
        out_shape=jax.ShapeDtypeStruct((b, t, w), BF16),
        scratch_shapes=[pltpu.VMEM((8, NA_W), F32)],
        compiler_params=_cparams(("arbitrary", "arbitrary")),
        name="na",
    )(q, k, v, k_ctx, v_ctx, eb, brange)


def _na_bias_tables(rpb, r, rows):
    nblk = rows // r
    nkr = r + NA_KH
    qc = np.arange(GRID_W)
    kc = np.arange(GRID_W)
    cs = np.clip(qc - NA_KW // 2, 0, GRID_W - NA_KW)
    col_ok = (kc[None, :] >= cs[:, None]) & (kc[None, :] < cs[:, None] + NA_KW)
    dc = kc[None, :] - qc[:, None] + NA_KW - 1
    col_oh = (dc[None] == np.arange(2 * NA_KW - 1)[:, None, None]) & col_ok[None]
    t1 = jnp.einsum("hij,jqk->hiqk", rpb, jnp.asarray(col_oh.astype(np.float32)), precision=lax.Precision.HIGHEST)
    t1 = t1 + jnp.asarray(np.where(col_ok, 0.0, NEG).astype(np.float32))
    neg_blk = jnp.full((NA_HEADS, GRID_W, GRID_W), NEG, F32)
    variants = []
    for blk in (0, min(1, nblk - 1), nblk - 1):
        r0 = blk * r
        lo = int(np.clip(r0 - NA_KH // 2, 0, rows - nkr))
        rows_out = []
        for a in range(r):
            qr = r0 + a
            rs = int(np.clip(qr - NA_KH // 2, 0, rows - NA_KH))
            blks = []
            for bk in range(nkr):
                kr = lo + bk
                blks.append(t1[:, kr - qr + NA_KH - 1] if rs <= kr < rs + NA_KH else neg_blk)
            rows_out.append(jnp.concatenate(blks, axis=-1))
        variants.append(jnp.concatenate(rows_out, axis=1))
    return jnp.stack(variants)


def _layer_norm(x, g, b):
    xc = x - jnp.mean(x, axis=-1, keepdims=True)
    return xc * lax.rsqrt(jnp.mean(xc * xc, axis=-1, keepdims=True) + NORM_EPS) * g + b


def _post_kernel(hc, x_ref, of_ref, ob_ref, sg_ref, gq_ref, na_ref, mod_ref, hn_ref, bd_ref, wo_ref, w1_ref, w2_ref,
                 ln_ref, o_ref):
    x = x_ref[0]
    g1, sh2, sc2, g2 = (mod_ref[0, k:k + 1, :] for k in range(4))
    o = jnp.concatenate([of_ref[0, c] + ob_ref[0, c] for c in range(HG_W // LANE)], axis=1)
    ms = _dot((o * o).astype(BF16), bd_ref[...])
    hg = (o * lax.rsqrt(ms + NORM_EPS) * hn_ref[...] * sg_ref[0]).astype(BF16)
    y = _dot(jnp.concatenate([hg, gq_ref[0], na_ref[0]], axis=1), wo_ref[...])
    x1 = _layer_norm(DN_ALPHA * x + g1 * y, ln_ref[0:1, :], ln_ref[1:2, :])
    h2 = (x1 * (1.0 + sc2) + sh2).astype(BF16)
    f, c0 = None, 0
    for width in hc:
        ug = _dot(h2, w1_ref[:, c0:c0 + width])
        uu = _dot(h2, w1_ref[:, FFN_HIDDEN + c0:FFN_HIDDEN + c0 + width])
        a = (ug * _sigmoid(ug) * uu).astype(BF16)
        part = _dot(a, w2_ref[c0:c0 + width, :])
        f = part if f is None else f + part
        c0 += width
    o_ref[0] = _layer_norm(DN_ALPHA * x1 + g2 * f, ln_ref[2:3, :], ln_ref[3:4, :])


def _post(x, o_f, o_b, sg, gq, na, mod4, hn, bd256, w_out, w1, w2, ln4, tm, hc):
    b, t, d = x.shape
    row = lambda bi, i: (bi, i, 0)
    const2 = lambda bi, i: (0, 0)
    return pl.pallas_call(
        functools.partial(_post_kernel, hc),
        grid=(b, t // tm),
        in_specs=[
            pl.BlockSpec((1, tm, d), row),
            pl.BlockSpec((1, HG_W // LANE, tm, LANE), lambda bi, i: (bi, 0, i, 0)),
            pl.BlockSpec((1, HG_W // LANE, tm, LANE), lambda bi, i: (bi, 0, i, 0)),
            pl.BlockSpec((1, tm, HG_W), row),
            pl.BlockSpec((1, tm, GQA_W), row),
            pl.BlockSpec((1, tm, NA_W), row),
            pl.BlockSpec((1, 4, d), lambda bi, i: (bi, 0, 0)),
            pl.BlockSpec((1, HG_W), const2),
            pl.BlockSpec((HG_W, HG_W), const2),
            pl.BlockSpec((d, d), const2),
            pl.BlockSpec((d, 2 * FFN_HIDDEN), const2),
            pl.BlockSpec((FFN_HIDDEN, d), const2),
            pl.BlockSpec((4, d), const2),
        ],
        out_specs=pl.BlockSpec((1, tm, d), row),
        out_shape=jax.ShapeDtypeStruct((b, t, d), F32),
        compiler_params=_cparams(("arbitrary", "arbitrary")),
        name="post",
    )(x, o_f, o_b, sg, gq, na, mod4, hn, bd256, w_out, w1, w2, ln4)


_PERM_MAIN = np.concatenate([np.arange(0, HEAD_DIM, 2), np.arange(1, HEAD_DIM, 2)])
_PERM_PART = np.concatenate([np.arange(1, HEAD_DIM, 2), np.arange(0, HEAD_DIM, 2)])


def _w_in_ext(w_in):
    w = w_in.astype(BF16)
    gq0 = 5 * HG_W
    gk0 = gq0 + GQA_W
    gv0 = gk0 + GQA_KV_W

    def head(base, h, odd_first):
        pairs = w[:, :, base + h * HEAD_DIM:base + (h + 1) * HEAD_DIM].reshape(w.shape[0], w.shape[1], HEAD_DIM // 2, 2)
        even, odd = pairs[..., 0], pairs[..., 1]
        return jnp.concatenate([odd, even] if odd_first else [even, odd], axis=-1)

    cols = [w[:, :, :gq0]]
    cols += [head(gq0, h, False) for h in GQA_SLAB_HEADS] + [head(gk0, h, False) for h in range(GQA_KV_HEADS)]
    cols += [w[:, :, gv0:]]
    cols += [head(gq0, h, True) for h in GQA_SLAB_HEADS] + [head(gk0, h, True) for h in range(GQA_KV_HEADS)]
    return jnp.concatenate(cols, axis=-1)


def _w_out_rows(w_out):
    g0 = HG_W
    blocks = [w_out[:, :g0]] + [w_out[:, g0 + h * HEAD_DIM:g0 + (h + 1) * HEAD_DIM] for h in GQA_SLAB_HEADS]
    return jnp.concatenate(blocks + [w_out[:, g0 + GQA_W:]], axis=1).astype(BF16)


def _rope_tables(n_tokens):
    t = np.arange(n_tokens)
    row = (t // GRID_W).astype(np.float32)
    col = (t % GRID_W).astype(np.float32)
    n_pairs = HEAD_DIM // 4
    inv_freq = np.exp(-math.log(ROPE_THETA) * np.arange(n_pairs, dtype=np.float32) / n_pairs).astype(np.float32)
    ang = np.concatenate([row[:, None] * inv_freq, col[:, None] * inv_freq], -1).astype(np.float32)
    ang = jnp.asarray(ang)
    c, s = jnp.cos(ang), jnp.sin(ang)
    cos = jnp.tile(jnp.concatenate([c, c], -1), (1, LANE // HEAD_DIM))
    sin = jnp.tile(jnp.concatenate([-s, s], -1), (1, LANE // HEAD_DIM))
    return cos, sin


def _block_diag_mean(width):
    c = np.arange(width)
    return jnp.asarray(((c[:, None] // HEAD_DIM) == (c[None, :] // HEAD_DIM)).astype(np.float32) / HEAD_DIM, BF16)


def kernel(x, c, ctx, c_ctx, w_ada, b_ada, w_in, hgrn_lb_logits, hgrn_norm, gqa_q_norm, gqa_k_norm, na_rpb, w_out,
           w_ffn_in, w_ffn_out, ln_g, ln_b):
    b, n, d = x.shape
    n_ctx = ctx.shape[1]
    rows = n // GRID_W
    tm = min(512, n)
    tm_ctx = min(256, n_ctx)
    tb = 256
    tq, tk = min(256, n), min(1024, n)
    na_r = 4
    hc = (1536, FFN_HIDDEN - 1536)

    cv = jnp.zeros((MOD_ROWS, d), F32).at[:b].set(c).at[b].set(c_ctx)
    mods = _mods(cv, w_ada, b_ada)

    w_ext = _w_in_ext(w_in)
    w_out_p = _w_out_rows(w_out)
    w1 = w_ffn_in.astype(BF16)
    w2 = w_ffn_out.astype(BF16)

    cos_l, sin_l = _rope_tables(n)
    cos_c = jnp.ones((n_ctx, LANE), F32)
    sin_c = jnp.zeros((n_ctx, LANE), F32)
    bd128 = _block_diag_mean(LANE)
    bd256 = _block_diag_mean(HG_W)
    hg_consts = _hgrn_consts()
    s_zero = jnp.zeros((b, 2, HG_W, HG_W), F32)

    x_lat, x_ctx = x, ctx
    for l in range(DEPTH):
        need_ctx = l < DEPTH - 1
        m = mods[l].reshape(MOD_ROWS, N_MOD, d)
        m_lat = m[:b]
        m_ctx = jnp.broadcast_to(m[b][None], (b, N_MOD, d))
        qscale = HEAD_DIM ** -0.5 * math.log2(math.e)
        gains = (
            (jnp.tile(gqa_q_norm[l][_PERM_MAIN], GQA_HEADS) * qscale)[None],
            (jnp.tile(gqa_q_norm[l][_PERM_PART], GQA_HEADS) * qscale)[None],
            jnp.tile(gqa_k_norm[l][_PERM_MAIN], GQA_KV_HEADS)[None],
            jnp.tile(gqa_k_norm[l][_PERM_PART], GQA_KV_HEADS)[None],
        )
        p_lat = _inproj(l, x_lat, m_lat[:, 0:1], m_lat[:, 1:2], w_ext[l], hgrn_lb_logits, cos_l, sin_l, gains, bd128, tm)
        p_ctx = _inproj(l, x_ctx, m_ctx[:, 0:1], m_ctx[:, 1:2], w_ext[l], hgrn_lb_logits, cos_c, sin_c, gains, bd128,
                        tm_ctx)
        hq_l, lf_l, hi_l, sg_l, gq_l, gk_l, gv_l, nq_l, nk_l, nv_l, gvt_l, gk2_l = p_lat
        hq_c, lf_c, hi_c, sg_c, gq_c, gk_c, gv_c, nq_c, nk_c, nv_c, gvt_c, gk2_c = p_ctx

        *o_c, s_c = _hgrn(hq_c, lf_c, hi_c, s_zero, hg_consts, min(tb, n_ctx))
        *o_l, _ = _hgrn(hq_l, lf_l, hi_l, s_c, hg_consts, tb)

        gqa_l = _gqa(gq_l, gk2_l, gvt_l, gk2_c, gvt_c, tq, tk)
        rpb2 = na_rpb[l] * math.log2(math.e)
        eb = _na_bias_tables(rpb2, na_r, rows)
        b_hi = jnp.maximum(jnp.max(rpb2, axis=(1, 2)), 0.0)
        b_lo = jnp.minimum(jnp.min(rpb2, axis=(1, 2)), 0.0)
        brange = jnp.broadcast_to(jnp.repeat(jnp.stack([b_hi, b_lo]), HEAD_DIM, axis=1)[:, None, :], (2, 8, NA_W))
        na_l = _na(nq_l, nk_l, nv_l, nk_c, nv_c, eb, brange, na_r)

        hn = jnp.tile(hgrn_norm[l], HG_HEADS)[None]
        ln4 = jnp.stack([ln_g[l, 0], ln_b[l, 0], ln_g[l, 1], ln_b[l, 1]])
        if need_ctx:
            gqa_c = _ctxattn(gq_c, gk_c, gv_c, True)
            na_c = _ctxattn(nq_c, nk_c, nv_c, False)
            x_ctx = _post(x_ctx, o_c[0], o_c[1], sg_c, gqa_c, na_c, m_ctx[:, 2:6], hn, bd256, w_out_p[l], w1[l], w2[l],
                          ln4, tm_ctx, hc)
        x_lat = _post(x_lat, o_l[0], o_l[1], sg_l, gqa_l, na_l, m_lat[:, 2:6], hn, bd256, w_out_p[l], w1[l], w2[l],
                      ln4, tm, hc)
    return x_lat
```

```python
import functools
import math

import numpy as np
import jax
import jax.numpy as jnp
from jax import lax
from jax.experimental import pallas as pl
from jax.experimental.pallas import tpu as pltpu

F32 = jnp.float32
BF16 = jnp.bfloat16

D_MODEL = 1024
DEPTH = 2
GRID_W = 64
HEAD_DIM = 64
HG_HEADS = 4
GQA_HEADS = 6
GQA_KV_HEADS = 2
NA_HEADS = 6
NA_KH = 8
NA_KW = 16
ROPE_THETA = 10000.0
NORM_EPS = 1e-6
N_MOD = 6
HG_W = HG_HEADS * HEAD_DIM
GQA_W = GQA_HEADS * HEAD_DIM
GQA_KV_W = GQA_KV_HEADS * HEAD_DIM
NA_W = NA_HEADS * HEAD_DIM
FFN_HIDDEN = 2816
DN_ALPHA = (2 * DEPTH) ** 0.25

LANE = 128
SUB = 16
MOD_ROWS = 8
NEG = -1e30
GQA_VT_ROWS = 80
GQA_SHIFT_LIMIT = 100.0
VMEM_LIMIT = 56 * 1024 * 1024

C_HQ, C_FF, C_FB, C_HI, C_HGATE = 0, 256, 512, 768, 1024
C_GQ, C_GK, C_GV = 1280, 1664, 1792
C_NQ, C_NK, C_NV = 1920, 2304, 2688
C_GQP, C_GKP = 3072, 3456
IN_EXT = 3584
GQA_SLAB_HEADS = (0, 3, 1, 4, 2, 5)


def _cparams(sem, flags=None):
    return pltpu.CompilerParams(dimension_semantics=sem, vmem_limit_bytes=VMEM_LIMIT, flags=flags)


def _sigmoid(z):
    return 1.0 / (1.0 + jnp.exp(-z))


def _dot(a, b):
    return jnp.dot(a, b, preferred_element_type=F32)


def _dot_nt(a, b):
    return lax.dot_general(a, b, (((1,), (1,)), ((), ())), preferred_element_type=F32)


def _mods_kernel(cv_ref, w_ref, b_ref, o_ref):
    cv = cv_ref[...]
    a = cv * _sigmoid(cv)
    o_ref[0] = jnp.dot(a, w_ref[0], preferred_element_type=F32, precision=lax.Precision.HIGHEST) + b_ref[0]


def _mods(cv, w_ada, b_ada):
    depth, d, nm = w_ada.shape
    tn = 1024
    return pl.pallas_call(
        _mods_kernel,
        grid=(depth, nm // tn),
        in_specs=[
            pl.BlockSpec((MOD_ROWS, d), lambda l, j: (0, 0)),
            pl.BlockSpec((1, d, tn), lambda l, j: (l, 0, j)),
            pl.BlockSpec((1, 1, tn), lambda l, j: (l, 0, j)),
        ],
        out_specs=pl.BlockSpec((1, MOD_ROWS, tn), lambda l, j: (l, 0, j)),
        out_shape=jax.ShapeDtypeStruct((depth, MOD_ROWS, nm), F32),
        compiler_params=_cparams(("arbitrary", "arbitrary")),
        name="mods",
    )(cv, w_ada, b_ada.reshape(depth, 1, nm))


def _inproj_kernel(layer, x_ref, sh_ref, sc_ref, w_ref, lbl_ref, cos_ref, sin_ref, gq_ref, gqp_ref, gk_ref, gkp_ref,
                   bd_ref, hq_ref, lf_ref, hi_ref, hg_ref, gq_o, gk_o, gv_o, nq_o, nk_o, nv_o, gvt_o, gk2_o):
    x = x_ref[0]
    h = (x * (1.0 + sc_ref[0]) + sh_ref[0]).astype(BF16)

    def proj(c0, width):
        return _dot(h, w_ref[:, c0:c0 + width])

    logits = [lbl_ref[d] for d in range(DEPTH)]
    mx = functools.reduce(jnp.maximum, logits)
    es = [jnp.exp(v - mx) for v in logits]
    inv = 1.0 / functools.reduce(lambda a, b: a + b, es)
    ps = [e * inv for e in es]
    lb = jnp.clip(functools.reduce(lambda a, b: a + b, ps[:layer + 1]) - ps[0], 0.0, 1.0)

    def put_halves(ref, first, a):
        for c in range(HG_W // LANE):
            ref[0, first + c] = a[:, c * LANE:(c + 1) * LANE]

    q = proj(C_HQ, HG_W)
    put_halves(hq_ref, 0, q * _sigmoid(q))
    for d, c0 in enumerate((C_FF, C_FB)):
        z = proj(c0, HG_W)
        lbd = lb[d:d + 1, :]
        put_halves(lf_ref, d * (HG_W // LANE), jnp.log(lbd + (1.0 - lbd) * _sigmoid(z)))
    put_halves(hi_ref, 0, proj(C_HI, HG_W))
    g = proj(C_HGATE, HG_W)
    hg_ref[0] = g * _sigmoid(g)

    cos = cos_ref[...]
    sin = sin_ref[...]
    bd = bd_ref[...]

    rot_w = GQA_W + GQA_KV_W
    main = proj(C_GQ, rot_w)
    part = proj(C_GQP, rot_w)
    nq_slabs = GQA_W // LANE
    for s in range(rot_w // LANE):
        sl = slice(s * LANE, (s + 1) * LANE)
        pm, pp = main[:, sl], part[:, sl]
        ms = _dot((pm * pm).astype(BF16), bd)
        r = lax.rsqrt(ms + NORM_EPS)
        if s < nq_slabs:
            gm, gp, out_ref, osl = gq_ref[:, sl], gqp_ref[:, sl], gq_o, sl
        else:
            osl = slice((s - nq_slabs) * LANE, (s - nq_slabs + 1) * LANE)
            gm, gp, out_ref = gk_ref[:, osl], gkp_ref[:, osl], gk_o
        out_ref[0, :, osl] = ((pm * r * gm) * cos + (pp * r * gp) * sin).astype(BF16)

    gv_nq = proj(C_GV, GQA_KV_W + NA_W)
    gv = gv_nq[:, :GQA_KV_W]
    gv_o[0] = gv.astype(BF16)
    nq_o[0] = (gv_nq[:, GQA_KV_W:] * (HEAD_DIM ** -0.5 * math.log2(math.e))).astype(BF16)
    nk_nv = proj(C_NK, 2 * NA_W)
    nk_o[0] = nk_nv[:, :NA_W].astype(BF16)
    nv_o[0] = nk_nv[:, NA_W:].astype(BF16)
    lane = lax.broadcasted_iota(jnp.int32, gv.shape, 1)
    for h, vh in enumerate((gv, pltpu.roll(gv, HEAD_DIM, 1))):
        vt = jnp.where(lane < HEAD_DIM, vh, jnp.where(lane == HEAD_DIM, 1.0, 0.0)).T
        gvt_o[0, h] = vt[:GQA_VT_ROWS].astype(BF16)
    gk = gk_o[0].astype(F32)
    gk2_o[0, 0] = jnp.where(lane < HEAD_DIM, gk, jnp.where(lane == HEAD_DIM, 1.0, 0.0)).astype(BF16)
    gk2_o[0, 1] = jnp.where(lane >= HEAD_DIM, gk, jnp.where(lane == 0, 1.0, 0.0)).astype(BF16)


def _inproj(layer, x, shift, scale, w_ext, lb_logits, cos, sin, gains, bd128, tm):
    b, t, d = x.shape
    gq, gqp, gk, gkp = gains
    row = lambda bi, i: (bi, i, 0)
    const2 = lambda bi, i: (0, 0)
    halves = (HG_W // LANE, 2 * HG_W // LANE, HG_W // LANE)
    widths = (HG_W, GQA_W, GQA_KV_W, GQA_KV_W, NA_W, NA_W, NA_W)
    dtypes = (F32, BF16, BF16, BF16, BF16, BF16, BF16)
    return pl.pallas_call(
        functools.partial(_inproj_kernel, layer),
        grid=(b, t // tm),
        in_specs=[
            pl.BlockSpec((1, tm, d), row),
            pl.BlockSpec((1, 1, d), lambda bi, i: (bi, 0, 0)),
            pl.BlockSpec((1, 1, d), lambda bi, i: (bi, 0, 0)),
            pl.BlockSpec((None, d, IN_EXT), lambda bi, i: (layer, 0, 0)),
            pl.BlockSpec((DEPTH, 2, HG_W), lambda bi, i: (0, 0, 0)),
            pl.BlockSpec((tm, LANE), lambda bi, i: (i, 0)),
            pl.BlockSpec((tm, LANE), lambda bi, i: (i, 0)),
            pl.BlockSpec((1, GQA_W), const2),
            pl.BlockSpec((1, GQA_W), const2),
            pl.BlockSpec((1, GQA_KV_W), const2),
            pl.BlockSpec((1, GQA_KV_W), const2),
            pl.BlockSpec((LANE, LANE), const2),
        ],
        out_specs=[pl.BlockSpec((1, nh, tm, LANE), lambda bi, i: (bi, 0, i, 0)) for nh in halves]
        + [pl.BlockSpec((1, tm, w), row) for w in widths]
        + [pl.BlockSpec((1, 2, GQA_VT_ROWS, tm), lambda bi, i: (bi, 0, 0, i)),
           pl.BlockSpec((1, 2, tm, LANE), lambda bi, i: (bi, 0, i, 0))],
        out_shape=[jax.ShapeDtypeStruct((b, nh, t, LANE), F32) for nh in halves]
        + [jax.ShapeDtypeStruct((b, t, w), dt) for w, dt in zip(widths, dtypes)]
        + [jax.ShapeDtypeStruct((b, 2, GQA_VT_ROWS, t), BF16), jax.ShapeDtypeStruct((b, 2, t, LANE), BF16)],
        compiler_params=_cparams(("arbitrary", "arbitrary")),
        name="inproj",
    )(x, shift, scale, w_ext, lb_logits, cos, sin, gq, gqp, gk, gkp, bd128)


def _hgrn_kernel(tb, qf_ref, lff_ref, vf_ref, qb_ref, lfb_ref, vb_ref, s0_ref, ind_ref, bm_ref,
                 of_ref, ob_ref, sfin_ref, st_scr, stb_scr, qs_scr, ks_scr):
    i = pl.program_id(1)
    nsc = tb // SUB

    @pl.when(i == 0)
    def _():
        st_scr[...] = s0_ref[0]
        stb_scr[...] = s0_ref[0].astype(BF16)

    ind = ind_ref[...]
    in_refs = ((qf_ref, lff_ref, vf_ref), (qb_ref, lfb_ref, vb_ref))
    o_refs = (of_ref, ob_ref)

    nh = HG_W // LANE

    def slab(tt):
        return pl.ds(tt, nsc, stride=SUB)

    def load(ref, lead, rows):
        return jnp.concatenate([ref[(*lead, c, rows, slice(None))] for c in range(nh)], axis=1)

    def store(ref, lead, rows, a, add=False):
        for c in range(nh):
            idx = (*lead, c, rows, slice(None))
            piece = a[:, c * LANE:(c + 1) * LANE]
            ref[idx] = ref[idx] + piece if add else piece

    orders = (list(range(SUB)), list(range(SUB - 1, -1, -1)))
    q, v, f, k, dec = [], [], [], [], []
    for d, (q_ref, lf_ref, v_ref) in enumerate(in_refs):
        lf = {tt: load(lf_ref, (0,), slab(tt)) for tt in orders[d]}
        q.append({tt: load(q_ref, (0,), slab(tt)) for tt in orders[d]})
        v.append({tt: load(v_ref, (0,), slab(tt)) for tt in orders[d]})
        f.append({tt: jnp.exp(lf[tt]) for tt in orders[d]})
        k.append({tt: 1.0 - f[d][tt] for tt in orders[d]})
        cum, run = {}, None
        for tt in orders[d]:
            run = lf[tt] if run is None else run + lf[tt]
            cum[tt] = run
        for tt in orders[d]:
            store(qs_scr, (d,), slab(tt), q[d][tt] * jnp.exp(cum[tt]))
            store(ks_scr, (d,), slab(tt), k[d][tt] * jnp.exp(run - cum[tt]))
        dec.append(jnp.exp(run))

    acc = [dict(), dict()]
    for a in range(SUB):
        for d in range(2):
            ss = orders[d][a]
            u = k[d][ss]
            es = []
            for tt in orders[d][a:]:
                if tt != ss:
                    u = u * f[d][tt]
                es.append((q[d][tt] * u).astype(BF16))
            att = _dot(jnp.concatenate(es, axis=0), ind)
            for n, tt in enumerate(orders[d][a:]):
                part = att[n * nsc:(n + 1) * nsc] * v[d][ss]
                acc[d][tt] = part if a == 0 else acc[d][tt] + part
    for d in range(2):
        for tt in orders[d]:
            store(o_refs[d], (0,), slab(tt), acc[d][tt])

    bm = bm_ref[...]
    for step in range(nsc):
        for d in range(2):
            j = nsc - 1 - step if d else step
            sl = slice(j * SUB, (j + 1) * SUB)
            store(o_refs[d], (0,), sl, _dot_nt(load(qs_scr, (d,), sl).astype(BF16), stb_scr[d]), add=True)
            kvt = lax.dot_general(load(in_refs[d][2], (0,), sl).astype(BF16), load(ks_scr, (d,), sl).astype(BF16),
                                  (((0,), (0,)), ((), ())), preferred_element_type=F32)
            for h in range(HG_HEADS):
                rs = slice(h * HEAD_DIM, (h + 1) * HEAD_DIM)
                ls = slice((h // 2) * LANE, (h // 2 + 1) * LANE)
                blk = st_scr[d, rs, ls] * dec[d][j:j + 1, ls] + kvt[rs, ls] * bm[rs, ls]
                st_scr[d, rs, ls] = blk
                stb_scr[d, rs, ls] = blk.astype(BF16)

    @pl.when(i == pl.num_programs(1) - 1)
    def _():
        sfin_ref[0] = st_scr[...]


def _hgrn(q, lf2, v, s0, consts, tb):
    b, nh, t, _ = q.shape
    nblk = t // tb
    ind, bm = consts
    fwd = lambda bi, i: (bi, 0, i, 0)
    bwd = lambda bi, i: (bi, 0, nblk - 1 - i, 0)
    bwd_lf = lambda bi, i: (bi, 1, nblk - 1 - i, 0)
    const2 = lambda bi, i: (0, 0)
    state = lambda bi, i: (bi, 0, 0, 0)
    blk = (1, nh, tb, LANE)
    return pl.pallas_call(
        functools.partial(_hgrn_kernel, tb),
        grid=(b, nblk),
        in_specs=[
            pl.BlockSpec(blk, fwd), pl.BlockSpec(blk, fwd), pl.BlockSpec(blk, fwd),
            pl.BlockSpec(blk, bwd), pl.BlockSpec(blk, bwd_lf), pl.BlockSpec(blk, bwd),
            pl.BlockSpec((1, 2, HG_W, HG_W), state),
            pl.BlockSpec((HG_W, HG_W), const2),
            pl.BlockSpec((HG_W, HG_W), const2),
        ],
        out_specs=[pl.BlockSpec(blk, fwd), pl.BlockSpec(blk, bwd), pl.BlockSpec((1, 2, HG_W, HG_W), state)],
        out_shape=[
            jax.ShapeDtypeStruct((b, nh, t, LANE), F32),
            jax.ShapeDtypeStruct((b, nh, t, LANE), F32),
            jax.ShapeDtypeStruct((b, 2, HG_W, HG_W), F32),
        ],
        scratch_shapes=[pltpu.VMEM((2, HG_W, HG_W), F32), pltpu.VMEM((2, HG_W, HG_W), BF16),
                        pltpu.VMEM((2, nh, tb, LANE), F32), pltpu.VMEM((2, nh, tb, LANE), F32)],
        compiler_params=_cparams(("arbitrary", "arbitrary")),
        name="hgrn",
    )(q, lf2, v, q, lf2, v, s0, ind, bm)


def _hgrn_consts():
    c = np.arange(HG_W)
    blk = (c[:, None] // HEAD_DIM) == (c[None, :] // HEAD_DIM)
    return jnp.asarray(blk.astype(np.float32), BF16), jnp.asarray(blk.astype(np.float32))


def _half_masks(rows):
    lane = lax.broadcasted_iota(jnp.int32, (rows, LANE), 1)
    return lane < HEAD_DIM, lane >= HEAD_DIM


def _sel(mask, a):
    return jnp.where(mask, a, jnp.zeros_like(a))


def _value_slabs(v):
    lane = lax.broadcasted_iota(jnp.int32, v.shape, 1)
    lo, hi = _half_masks(v.shape[0])
    e_lo = jnp.where(lane == HEAD_DIM, 1.0, 0.0).astype(v.dtype)
    e_hi = jnp.where(lane == 0, 1.0, 0.0).astype(v.dtype)
    return _sel(lo, v) + e_lo, _sel(hi, v) + e_hi


def _normalize_slab(a_lo, a_hi):
    lane = lax.broadcasted_iota(jnp.int32, a_lo.shape, 1)
    return jnp.where(lane < HEAD_DIM, a_lo / a_lo[:, HEAD_DIM:HEAD_DIM + 1], a_hi / a_hi[:, 0:1])


def _gqa_kernel(tq, tk, n_lat, q_ref, kl_ref, vtl_ref, kc_ref, vtc_ref, o_ref, kmax_scr, m_scr, acc_scr):
    i = pl.program_id(1)
    nslab = GQA_W // LANE
    n_chunks = n_lat // tk
    heads = [(s, half) for s in range(nslab) for half in range(2)]

    def key_chunk(c):
        start = pl.multiple_of(c * tk, tk)
        return ([kl_ref[0, h, pl.ds(start, tk), :] for h in range(2)],
                [vtl_ref[0, h, :, pl.ds(start, tk)] for h in range(2)])

    @pl.when(i == 0)
    def _():
        for h in range(2):
            def norm2(kb):
                x = kb.astype(F32)
                lane = lax.broadcasted_iota(jnp.int32, x.shape, 1)
                x = jnp.where((lane < HEAD_DIM) if h == 0 else (lane >= HEAD_DIM), x, 0.0)
                mx = jnp.max(jnp.sum(x * x, axis=1, keepdims=True), axis=0, keepdims=True)
                return jnp.broadcast_to(mx, kmax_scr.shape[1:])

            mx = lax.fori_loop(0, n_chunks, lambda c, m: jnp.maximum(m, norm2(key_chunk(c)[0][h])),
                               norm2(kc_ref[0, h]))
            kmax_scr[h] = jnp.sqrt(mx)

    qlo, qhi = _half_masks(tq)
    lane = lax.broadcasted_iota(jnp.int32, (tq, LANE), 1)
    q_all = q_ref[0].astype(F32)
    rr = lax.broadcasted_iota(jnp.int32, (GQA_W, GQA_W), 0) // HEAD_DIM
    cc = lax.broadcasted_iota(jnp.int32, (GQA_W, GQA_W), 1) // HEAD_DIM
    qn2 = _dot((q_all * q_all).astype(BF16), jnp.where(rr == cc, 1.0, 0.0).astype(BF16))
    kmax = jnp.where(lane[0:1] < HEAD_DIM, kmax_scr[0][0:1], kmax_scr[1][0:1])
    bound = jnp.sqrt(qn2) * jnp.concatenate([kmax] * nslab, axis=1)
    use_bound = 2.0 * jnp.max(bound) < GQA_SHIFT_LIMIT

    def query(idx, bound_path):
        s, half = heads[idx]
        qf = q_all[:, s * LANE:(s + 1) * LANE]
        if not bound_path:
            return jnp.where(qhi if half else qlo, qf, 0.0).astype(BF16)
        swapped = pltpu.roll(bound[:, s * LANE:(s + 1) * LANE], HEAD_DIM, 1)
        return jnp.where(lane == (0 if half else HEAD_DIM), -swapped, qf).astype(BF16)

    qshift = [query(idx, True) for idx in range(len(heads))]

    def step_bound(k2, vt):
        sts = [_dot_nt(k2[half], qshift[idx]) for idx, (s, half) in enumerate(heads)]
        for idx, (s, half) in enumerate(heads):
            acc_scr[idx] = acc_scr[idx] + _dot(vt[half], jnp.exp2(sts[idx]).astype(BF16))

    def step_exact(qh, k2, vt):
        sts = [_dot_nt(k2[half], qh[idx]) for idx, (s, half) in enumerate(heads)]
        for idx, (s, half) in enumerate(heads):
            m_old = m_scr[idx]
            m_new = jnp.maximum(m_old, jnp.max(sts[idx], axis=0, keepdims=True))
            pt = jnp.exp2(sts[idx] - m_new).astype(BF16)
            acc_scr[idx] = jnp.exp2(m_old - m_new) * acc_scr[idx] + _dot(vt[half], pt)
            m_scr[idx] = m_new

    def run(step):
        def body(c, carry):
            step(*key_chunk(c))
            return carry

        lax.fori_loop(0, n_chunks, body, 0)
        step([kc_ref[0, h] for h in range(2)], [vtc_ref[0, h] for h in range(2)])

    acc_scr[...] = jnp.zeros(acc_scr.shape, F32)

    @pl.when(use_bound)
    def _():
        run(step_bound)

    @pl.when(jnp.logical_not(use_bound))
    def _():
        m_scr[...] = jnp.full(m_scr.shape, -jnp.inf, F32)
        run(functools.partial(step_exact, [query(idx, False) for idx in range(len(heads))]))

    for s in range(nslab):
        halves = [acc_scr[2 * s + half] for half in range(2)]
        out_t = jnp.concatenate([a[:HEAD_DIM] / a[HEAD_DIM:HEAD_DIM + 1] for a in halves], axis=0)
        o_ref[0, :, s * LANE:(s + 1) * LANE] = out_t.T.astype(BF16)


def _gqa(q, k_lat, vt_lat, k_ctx, vt_ctx, tq, tk):
    b, t, _ = q.shape
    n_lat = k_lat.shape[2]
    n_ctx = k_ctx.shape[2]
    full4 = lambda bi, i: (bi, 0, 0, 0)
    return pl.pallas_call(
        functools.partial(_gqa_kernel, tq, tk, n_lat),
        grid=(b, t // tq),
        in_specs=[
            pl.BlockSpec((1, tq, GQA_W), lambda bi, i: (bi, i, 0)),
            pl.BlockSpec((1, 2, n_lat, LANE), full4),
            pl.BlockSpec((1, 2, GQA_VT_ROWS, n_lat), full4),
            pl.BlockSpec((1, 2, n_ctx, LANE), full4),
            pl.BlockSpec((1, 2, GQA_VT_ROWS, n_ctx), full4),
        ],
        out_specs=pl.BlockSpec((1, tq, GQA_W), lambda bi, i: (bi, i, 0)),
        out_shape=jax.ShapeDtypeStruct((b, t, GQA_W), BF16),
        scratch_shapes=[
            pltpu.VMEM((GQA_KV_HEADS, 8, LANE), F32),
            pltpu.VMEM((GQA_HEADS, 1, tq), F32),
            pltpu.VMEM((GQA_HEADS, GQA_VT_ROWS, tq), F32),
        ],
        compiler_params=_cparams(("arbitrary", "arbitrary")),
        name="gqa",
    )(q, k_lat, vt_lat, k_ctx, vt_ctx)


def _ctxattn_kernel(nslab, shared_kv, q_ref, k_ref, v_ref, o_ref):
    n = q_ref.shape[1]
    lo, hi = _half_masks(n)
    for s in range(nslab):
        ks = 0 if shared_kv else s
        q = q_ref[0, :, s * LANE:(s + 1) * LANE]
        k = k_ref[0, :, ks * LANE:(ks + 1) * LANE]
        v = v_ref[0, :, ks * LANE:(ks + 1) * LANE]
        out = None
        for mask in (lo, hi):
            sc = _dot_nt(_sel(mask, q), k)
            p = jnp.exp2(sc - jnp.max(sc, axis=1, keepdims=True))
            o_h = _dot(p.astype(BF16), _sel(mask, v)) / jnp.sum(p, axis=1, keepdims=True)
            out = o_h if out is None else out + o_h
        o_ref[0, :, s * LANE:(s + 1) * LANE] = out.astype(BF16)


def _ctxattn(q, k, v, shared_kv):
    b, n, w = q.shape
    kw = k.shape[2]
    full = lambda bi: (bi, 0, 0)
    return pl.pallas_call(
        functools.partial(_ctxattn_kernel, w // LANE, shared_kv),
        grid=(b,),
        in_specs=[pl.BlockSpec((1, n, w), full), pl.BlockSpec((1, n, kw), full), pl.BlockSpec((1, n, kw), full)],
        out_specs=pl.BlockSpec((1, n, w), full),
        out_shape=jax.ShapeDtypeStruct((b, n, w), BF16),
        compiler_params=_cparams(("arbitrary",)),
        name="ctxattn",
    )(q, k, v)


def _na_window_start(i, r, rows):
    return jnp.clip(i * r - NA_KH // 2, 0, rows - (r + NA_KH))


def _na_kernel(r, rows, q_ref, k_ref, v_ref, kc_ref, vc_ref, eb_ref, br_ref, o_ref, kmax_scr):
    i = pl.program_id(1)
    tq = r * GRID_W
    kwin = (r + NA_KH) * GRID_W
    n_lat = k_ref.shape[1]
    n_ctx = kc_ref.shape[1]
    heads = [(s, half) for s in range(NA_W // LANE) for half in range(2)]

    @pl.when(i == 0)
    def _():
        rr = lax.broadcasted_iota(jnp.int32, (NA_W, NA_W), 0) // HEAD_DIM
        cc = lax.broadcasted_iota(jnp.int32, (NA_W, NA_W), 1) // HEAD_DIM
        ones_bd = jnp.where(rr == cc, 1.0, 0.0).astype(BF16)

        def norm2(kb):
            x = kb.astype(F32)
            n2 = _dot((x * x).astype(BF16), ones_bd)
            return jnp.broadcast_to(jnp.max(n2, axis=0, keepdims=True), kmax_scr.shape)

        ck = min(1024, n_lat)
        mx = lax.fori_loop(0, n_lat // ck,
                           lambda c, m: jnp.maximum(m, norm2(k_ref[0, pl.ds(pl.multiple_of(c * ck, ck), ck), :])),
                           norm2(kc_ref[0]))
        kmax_scr[...] = jnp.sqrt(mx)

    start = pl.multiple_of(_na_window_start(i, r, rows) * GRID_W, GRID_W)
    lane = lax.broadcasted_iota(jnp.int32, (tq, LANE), 1)
    qlo, qhi = _half_masks(tq)

    def ones_lane(n, half, dtype):
        ln = lax.broadcasted_iota(jnp.int32, (n, LANE), 1)
        return jnp.where(ln == (0 if half else HEAD_DIM), 1.0, 0.0).astype(dtype)

    q_all = q_ref[0].astype(F32)
    rr = lax.broadcasted_iota(jnp.int32, (NA_W, NA_W), 0) // HEAD_DIM
    cc = lax.broadcasted_iota(jnp.int32, (NA_W, NA_W), 1) // HEAD_DIM
    qn2 = _dot((q_all * q_all).astype(BF16), jnp.where(rr == cc, 1.0, 0.0).astype(BF16))
    reach = jnp.sqrt(qn2) * kmax_scr[0:1, :]
    shift = reach + br_ref[0][0:1, :]
    use_bound = jnp.max(2.0 * reach + (br_ref[0][0:1, :] - br_ref[1][0:1, :])) < GQA_SHIFT_LIMIT
    def query(h, bound_path):
        s, half = heads[h]
        qf = q_all[:, s * LANE:(s + 1) * LANE]
        if not bound_path:
            return jnp.where(qhi if half else qlo, qf, 0.0).astype(BF16)
        swapped = pltpu.roll(shift[:, s * LANE:(s + 1) * LANE], HEAD_DIM, 1)
        return jnp.where(lane == (0 if half else HEAD_DIM), -swapped, qf).astype(BF16)

    def attend(bound_path):
        for s in range(NA_W // LANE):
            sl = slice(s * LANE, (s + 1) * LANE)
            kw = k_ref[0, pl.ds(start, kwin), sl]
            kc = kc_ref[0, :, sl]
            vws = _value_slabs(v_ref[0, pl.ds(start, kwin), sl])
            vcs = _value_slabs(vc_ref[0, :, sl])
            wm, cm = _half_masks(kwin), _half_masks(n_ctx)
            outs = []
            for half in range(2):
                h = 2 * s + half
                qv = query(h, bound_path)
                if bound_path:
                    s_loc = _dot_nt(qv, _sel(wm[half], kw) + ones_lane(kwin, half, BF16)) + eb_ref[0, h]
                    s_ctx = _dot_nt(qv, _sel(cm[half], kc) + ones_lane(n_ctx, half, BF16))
                else:
                    s_loc = _dot_nt(qv, kw) + eb_ref[0, h]
                    s_ctx = _dot_nt(qv, kc)
                    m = jnp.maximum(jnp.max(s_loc, axis=1, keepdims=True), jnp.max(s_ctx, axis=1, keepdims=True))
                    s_loc, s_ctx = s_loc - m, s_ctx - m
                outs.append(_dot(jnp.exp2(s_loc).astype(BF16), vws[half])
                            + _dot(jnp.exp2(s_ctx).astype(BF16), vcs[half]))
            o_ref[0, :, sl] = _normalize_slab(outs[0], outs[1]).astype(BF16)

    @pl.when(use_bound)
    def _():
        attend(True)

    @pl.when(jnp.logical_not(use_bound))
    def _():
        attend(False)


def _na(q, k, v, k_ctx, v_ctx, eb, brange, r):
    b, t, w = q.shape
    rows = t // GRID_W
    nblk = rows // r
    tq = r * GRID_W
    kwin = (r + NA_KH) * GRID_W
    n_ctx = k_ctx.shape[1]
    full = lambda bi, i: (bi, 0, 0)

    def variant(bi, i):
        return (jnp.where(i == 0, 0, jnp.where(i == nblk - 1, 2, 1)), 0, 0, 0)

    return pl.pallas_call(
        functools.partial(_na_kernel, r, rows),
        grid=(b, nblk),
        in_specs=[
            pl.BlockSpec((1, tq, w), lambda bi, i: (bi, i, 0)),
            pl.BlockSpec((1, t, w), full),
            pl.BlockSpec((1, t, w), full),
            pl.BlockSpec((1, n_ctx, w), full),
            pl.BlockSpec((1, n_ctx, w), full),
            pl.BlockSpec((1, NA_HEADS, tq, kwin), variant),
            pl.BlockSpec((2, 8, NA_W), lambda bi, i: (0, 0, 0)),
        ],
        out_specs=pl.BlockSpec((1, tq, w), lambda bi, i: (bi, i, 0)),
        out_shape=jax.ShapeDtypeStruct((b, t, w), BF16),
        scratch_shapes=[pltpu.VMEM((8, NA_W), F32)],
        compiler_params=_cparams(("arbitrary", "arbitrary")),
        name="na",
    )(q, k, v, k_ctx, v_ctx, eb, brange)


def _na_bias_tables(rpb, r, rows):
    nblk = rows // r
    nkr = r + NA_KH
    qc = np.arange(GRID_W)
    kc = np.arange(GRID_W)
    cs = np.clip(qc - NA_KW // 2, 0, GRID_W - NA_KW)
    col_ok = (kc[None, :] >= cs[:, None]) & (kc[None, :] < cs[:, None] + NA_KW)
    dc = kc[None, :] - qc[:, None] + NA_KW - 1
    col_oh = (dc[None] == np.arange(2 * NA_KW - 1)[:, None, None]) & col_ok[None]
    t1 = jnp.einsum("hij,jqk->hiqk", rpb, jnp.asarray(col_oh.astype(np.float32)), precision=lax.Precision.HIGHEST)
    t1 = t1 + jnp.asarray(np.where(col_ok, 0.0, NEG).astype(np.float32))
    neg_blk = jnp.full((NA_HEADS, GRID_W, GRID_W), NEG, F32)
    variants = []
    for blk in (0, min(1, nblk - 1), nblk - 1):
        r0 = blk * r
        lo = int(np.clip(r0 - NA_KH // 2, 0, rows - nkr))
        rows_out = []
        for a in range(r):
            qr = r0 + a
            rs = int(np.clip(qr - NA_KH // 2, 0, rows - NA_KH))
            blks = []
            for bk in range(nkr):
                kr = lo + bk
                blks.append(t1[:, kr - qr + NA_KH - 1] if rs <= kr < rs + NA_KH else neg_blk)
            rows_out.append(jnp.concatenate(blks, axis=-1))
        variants.append(jnp.concatenate(rows_out, axis=1))
    return jnp.stack(variants)


def _layer_norm(x, g, b):
    xc = x - jnp.mean(x, axis=-1, keepdims=True)
    return xc * lax.rsqrt(jnp.mean(xc * xc, axis=-1, keepdims=True) + NORM_EPS) * g + b


def _post_kernel(hc, x_ref, of_ref, ob_ref, sg_ref, gq_ref, na_ref, mod_ref, hn_ref, bd_ref, wo_ref, w1_ref, w2_ref,
                 ln_ref, o_ref):
    x = x_ref[0]
    g1, sh2, sc2, g2 = (mod_ref[0, k:k + 1, :] for k in range(4))
    o = jnp.concatenate([of_ref[0, c] + ob_ref[0, c] for c in range(HG_W // LANE)], axis=1)
    ms = _dot((o * o).astype(BF16), bd_ref[...])
    hg = (o * lax.rsqrt(ms + NORM_EPS) * hn_ref[...] * sg_ref[0]).astype(BF16)
    y = _dot(jnp.concatenate([hg, gq_ref[0], na_ref[0]], axis=1), wo_ref[...])
    x1 = _layer_norm(DN_ALPHA * x + g1 * y, ln_ref[0:1, :], ln_ref[1:2, :])
    h2 = (x1 * (1.0 + sc2) + sh2).astype(BF16)
    f, c0 = None, 0
    for width in hc:
        ug = _dot(h2, w1_ref[:, c0:c0 + width])
        uu = _dot(h2, w1_ref[:, FFN_HIDDEN + c0:FFN_HIDDEN + c0 + width])
        a = (ug * _sigmoid(ug) * uu).astype(BF16)
        part = _dot(a, w2_ref[c0:c0 + width, :])
        f = part if f is None else f + part
        c0 += width
    o_ref[0] = _layer_norm(DN_ALPHA * x1 + g2 * f, ln_ref[2:3, :], ln_ref[3:4, :])


def _post(layer, x, o_f, o_b, sg, gq, na, mod4, hn, bd256, w_out, w1, w2, ln4, tm, hc):
    b, t, d = x.shape
    row = lambda bi, i: (bi, i, 0)
    const2 = lambda bi, i: (0, 0)
    per_layer = lambda bi, i: (layer, 0, 0)
    return pl.pallas_call(
        functools.partial(_post_kernel, hc),
        grid=(b, t // tm),
        in_specs=[
            pl.BlockSpec((1, tm, d), row),
            pl.BlockSpec((1, HG_W // LANE, tm, LANE), lambda bi, i: (bi, 0, i, 0)),
            pl.BlockSpec((1, HG_W // LANE, tm, LANE), lambda bi, i: (bi, 0, i, 0)),
            pl.BlockSpec((1, tm, HG_W), row),
            pl.BlockSpec((1, tm, GQA_W), row),
            pl.BlockSpec((1, tm, NA_W), row),
            pl.BlockSpec((1, 4, d), lambda bi, i: (bi, 0, 0)),
            pl.BlockSpec((1, HG_W), const2),
            pl.BlockSpec((HG_W, HG_W), const2),
            pl.BlockSpec((None, d, d), per_layer),
            pl.BlockSpec((None, d, 2 * FFN_HIDDEN), per_layer),
            pl.BlockSpec((None, FFN_HIDDEN, d), per_layer),
            pl.BlockSpec((4, d), const2),
        ],
        out_specs=pl.BlockSpec((1, tm, d), row),
        out_shape=jax.ShapeDtypeStruct((b, t, d), F32),
        compiler_params=_cparams(("arbitrary", "arbitrary")),
        name="post",
    )(x, o_f, o_b, sg, gq, na, mod4, hn, bd256, w_out, w1, w2, ln4)


_PERM_MAIN = np.concatenate([np.arange(0, HEAD_DIM, 2), np.arange(1, HEAD_DIM, 2)])
_PERM_PART = np.concatenate([np.arange(1, HEAD_DIM, 2), np.arange(0, HEAD_DIM, 2)])


def _w_in_ext(w_in):
    w = w_in.astype(BF16)
    gq0 = 5 * HG_W
    gk0 = gq0 + GQA_W
    gv0 = gk0 + GQA_KV_W

    def head(base, h, odd_first):
        pairs = w[:, :, base + h * HEAD_DIM:base + (h + 1) * HEAD_DIM].reshape(w.shape[0], w.shape[1], HEAD_DIM // 2, 2)
        even, odd = pairs[..., 0], pairs[..., 1]
        return jnp.concatenate([odd, even] if odd_first else [even, odd], axis=-1)

    cols = [w[:, :, :gq0]]
    cols += [head(gq0, h, False) for h in GQA_SLAB_HEADS] + [head(gk0, h, False) for h in range(GQA_KV_HEADS)]
    cols += [w[:, :, gv0:]]
    cols += [head(gq0, h, True) for h in GQA_SLAB_HEADS] + [head(gk0, h, True) for h in range(GQA_KV_HEADS)]
    return jnp.concatenate(cols, axis=-1)


def _w_out_rows(w_out):
    g0 = HG_W
    blocks = [w_out[:, :g0]] + [w_out[:, g0 + h * HEAD_DIM:g0 + (h + 1) * HEAD_DIM] for h in GQA_SLAB_HEADS]
    return jnp.concatenate(blocks + [w_out[:, g0 + GQA_W:]], axis=1).astype(BF16)


def _rope_tables(n_tokens):
    t = np.arange(n_tokens)
    row = (t // GRID_W).astype(np.float32)
    col = (t % GRID_W).astype(np.float32)
    n_pairs = HEAD_DIM // 4
    inv_freq = np.exp(-math.log(ROPE_THETA) * np.arange(n_pairs, dtype=np.float32) / n_pairs).astype(np.float32)
    ang = np.concatenate([row[:, None] * inv_freq, col[:, None] * inv_freq], -1).astype(np.float32)
    ang = jnp.asarray(ang)
    c, s = jnp.cos(ang), jnp.sin(ang)
    cos = jnp.tile(jnp.concatenate([c, c], -1), (1, LANE // HEAD_DIM))
    sin = jnp.tile(jnp.concatenate([-s, s], -1), (1, LANE // HEAD_DIM))
    return cos, sin


def _block_diag_mean(width):
    c = np.arange(width)
    return jnp.asarray(((c[:, None] // HEAD_DIM) == (c[None, :] // HEAD_DIM)).astype(np.float32) / HEAD_DIM, BF16)


def kernel(x, c, ctx, c_ctx, w_ada, b_ada, w_in, hgrn_lb_logits, hgrn_norm, gqa_q_norm, gqa_k_norm, na_rpb, w_out,
           w_ffn_in, w_ffn_out, ln_g, ln_b):
    b, n, d = x.shape
    n_ctx = ctx.shape[1]
    rows = n // GRID_W
    tm = min(512, n)
    tm_ctx = min(256, n_ctx)
    tb = 256
    tq, tk = min(512, n), min(2048, n)
    na_r = 4
    hc = (1536, FFN_HIDDEN - 1536)

    cv = jnp.zeros((MOD_ROWS, d), F32).at[:b].set(c).at[b].set(c_ctx)
    mods = _mods(cv, w_ada, b_ada)

    w_ext = _w_in_ext(w_in)
    w_out_p = _w_out_rows(w_out)
    w1 = w_ffn_in.astype(BF16)
    w2 = w_ffn_out.astype(BF16)

    cos_l, sin_l = _rope_tables(n)
    cos_c = jnp.ones((n_ctx, LANE), F32)
    sin_c = jnp.zeros((n_ctx, LANE), F32)
    bd128 = _block_diag_mean(LANE)
    bd256 = _block_diag_mean(HG_W)
    hg_consts = _hgrn_consts()
    s_zero = jnp.zeros((b, 2, HG_W, HG_W), F32)

    x_lat, x_ctx = x, ctx
    for l in range(DEPTH):
        need_ctx = l < DEPTH - 1
        m = mods[l].reshape(MOD_ROWS, N_MOD, d)
        m_lat = m[:b]
        m_ctx = jnp.broadcast_to(m[b][None], (b, N_MOD, d))
        qscale = HEAD_DIM ** -0.5 * math.log2(math.e)
        gains = (
            (jnp.tile(gqa_q_norm[l][_PERM_MAIN], GQA_HEADS) * qscale)[None],
            (jnp.tile(gqa_q_norm[l][_PERM_PART], GQA_HEADS) * qscale)[None],
            jnp.tile(gqa_k_norm[l][_PERM_MAIN], GQA_KV_HEADS)[None],
            jnp.tile(gqa_k_norm[l][_PERM_PART], GQA_KV_HEADS)[None],
        )
        p_lat = _inproj(l, x_lat, m_lat[:, 0:1], m_lat[:, 1:2], w_ext, hgrn_lb_logits, cos_l, sin_l, gains, bd128, tm)
        p_ctx = _inproj(l, x_ctx, m_ctx[:, 0:1], m_ctx[:, 1:2], w_ext, hgrn_lb_logits, cos_c, sin_c, gains, bd128,
                        tm_ctx)
        hq_l, lf_l, hi_l, sg_l, gq_l, gk_l, gv_l, nq_l, nk_l, nv_l, gvt_l, gk2_l = p_lat
        hq_c, lf_c, hi_c, sg_c, gq_c, gk_c, gv_c, nq_c, nk_c, nv_c, gvt_c, gk2_c = p_ctx

        *o_c, s_c = _hgrn(hq_c, lf_c, hi_c, s_zero, hg_consts, min(tb, n_ctx))
        *o_l, _ = _hgrn(hq_l, lf_l, hi_l, s_c, hg_consts, tb)

        gqa_l = _gqa(gq_l, gk2_l, gvt_l, gk2_c, gvt_c, tq, tk)
        rpb2 = na_rpb[l] * math.log2(math.e)
        eb = _na_bias_tables(rpb2, na_r, rows)
        b_hi = jnp.maximum(jnp.max(rpb2, axis=(1, 2)), 0.0)
        b_lo = jnp.minimum(jnp.min(rpb2, axis=(1, 2)), 0.0)
        brange = jnp.broadcast_to(jnp.repeat(jnp.stack([b_hi, b_lo]), HEAD_DIM, axis=1)[:, None, :], (2, 8, NA_W))
        na_l = _na(nq_l, nk_l, nv_l, nk_c, nv_c, eb, brange, na_r)

        hn = jnp.tile(hgrn_norm[l], HG_HEADS)[None]
        ln4 = jnp.stack([ln_g[l, 0], ln_b[l, 0], ln_g[l, 1], ln_b[l, 1]])
        if need_ctx:
            gqa_c = _ctxattn(gq_c, gk_c, gv_c, True)
            na_c = _ctxattn(nq_c, nk_c, nv_c, False)
            x_ctx = _post(l, x_ctx, o_c[0], o_c[1], sg_c, gqa_c, na_c, m_ctx[:, 2:6], hn, bd256, w_out_p, w1, w2,
                          ln4, tm_ctx, hc)
        x_lat = _post(l, x_lat, o_l[0], o_l[1], sg_l, gqa_l, na_l, m_lat[:, 2:6], hn, bd256, w_out_p, w1, w2,
                      ln4, tm, hc)
    return x_lat
```

```python
import functools
import math

import numpy as np
import jax
import jax.numpy as jnp
from jax import lax
from jax.experimental import pallas as pl
from jax.experimental.pallas import tpu as pltpu

F32 = jnp.float32
BF16 = jnp.bfloat16

D_MODEL = 1024
DEPTH = 2
GRID_W = 64
HEAD_DIM = 64
HG_HEADS = 4
GQA_HEADS = 6
GQA_KV_HEADS = 2
NA_HEADS = 6
NA_KH = 8
NA_KW = 16
ROPE_THETA = 10000.0
NORM_EPS = 1e-6
N_MOD = 6
HG_W = HG_HEADS * HEAD_DIM
GQA_W = GQA_HEADS * HEAD_DIM
GQA_KV_W = GQA_KV_HEADS * HEAD_DIM
NA_W = NA_HEADS * HEAD_DIM
FFN_HIDDEN = 2816
DN_ALPHA = (2 * DEPTH) ** 0.25

LANE = 128
SUB = 16
MOD_ROWS = 8
NEG = -1e30
POST_PARTS = 2
GQA_VT_ROWS = 80
GQA_SHIFT_LIMIT = 100.0
VMEM_LIMIT = 56 * 1024 * 1024

C_HQ, C_FF, C_FB, C_HI, C_HGATE = 0, 256, 512, 768, 1024
C_GV, C_NK = 1792, 2304
GQA_SLAB_HEADS = (0, 3, 1, 4, 2, 5)


def _cparams(sem, flags=None):
    return pltpu.CompilerParams(dimension_semantics=sem, vmem_limit_bytes=VMEM_LIMIT, flags=flags)


def _sigmoid(z):
    return 1.0 / (1.0 + jnp.exp(-z))


def _dot(a, b):
    return jnp.dot(a, b, preferred_element_type=F32)


def _dot_nt(a, b):
    return lax.dot_general(a, b, (((1,), (1,)), ((), ())), preferred_element_type=F32)


def _mods_kernel(cv_ref, w_ref, b_ref, o_ref):
    cv = cv_ref[...]
    a = cv * _sigmoid(cv)
    o_ref[0] = jnp.dot(a, w_ref[0], preferred_element_type=F32, precision=lax.Precision.HIGHEST) + b_ref[0]


def _mods(cv, w_ada, b_ada):
    depth, d, nm = w_ada.shape
    tn = 1024
    return pl.pallas_call(
        _mods_kernel,
        grid=(depth, nm // tn),
        in_specs=[
            pl.BlockSpec((MOD_ROWS, d), lambda l, j: (0, 0)),
            pl.BlockSpec((1, d, tn), lambda l, j: (l, 0, j)),
            pl.BlockSpec((1, 1, tn), lambda l, j: (l, 0, j)),
        ],
        out_specs=pl.BlockSpec((1, MOD_ROWS, tn), lambda l, j: (l, 0, j)),
        out_shape=jax.ShapeDtypeStruct((depth, MOD_ROWS, nm), F32),
        compiler_params=_cparams(("arbitrary", "arbitrary")),
        name="mods",
    )(cv, w_ada, b_ada.reshape(depth, 1, nm))


def _inproj_kernel(layer, x_ref, sh_ref, sc_ref, w_ref, wr_ref, lbl_ref, cos_ref, sin_ref, gq_ref, gqp_ref, gk_ref, gkp_ref,
                   bd_ref, hq_ref, lf_ref, hi_ref, hg_ref, gq_o, gk_o, gv_o, nq_o, nk_o, nv_o, gvt_o, gk2_o):
    x = x_ref[0]
    h = (x * (1.0 + sc_ref[0]) + sh_ref[0]).astype(BF16)

    def proj(c0, width):
        return _dot(h, w_ref[:, c0:c0 + width])

    logits = [lbl_ref[d] for d in range(DEPTH)]
    mx = functools.reduce(jnp.maximum, logits)
    es = [jnp.exp(v - mx) for v in logits]
    inv = 1.0 / functools.reduce(lambda a, b: a + b, es)
    ps = [e * inv for e in es]
    lb = jnp.clip(functools.reduce(lambda a, b: a + b, ps[:layer + 1]) - ps[0], 0.0, 1.0)

    def put_halves(ref, first, a):
        for c in range(HG_W // LANE):
            ref[0, first + c] = a[:, c * LANE:(c + 1) * LANE]

    q = proj(C_HQ, HG_W)
    put_halves(hq_ref, 0, q * _sigmoid(q))
    for d, c0 in enumerate((C_FF, C_FB)):
        z = proj(c0, HG_W)
        lbd = lb[d:d + 1, :]
        put_halves(lf_ref, d * (HG_W // LANE), jnp.log(lbd + (1.0 - lbd) * _sigmoid(z)))
    put_halves(hi_ref, 0, proj(C_HI, HG_W))
    g = proj(C_HGATE, HG_W)
    hg_ref[0] = g * _sigmoid(g)

    cos = cos_ref[...]
    sin = sin_ref[...]
    bd = bd_ref[...]

    rot_w = GQA_W + GQA_KV_W
    main = _dot(h, wr_ref[:, :rot_w])
    part = _dot(h, wr_ref[:, rot_w:])
    nq_slabs = GQA_W // LANE
    for s in range(rot_w // LANE):
        sl = slice(s * LANE, (s + 1) * LANE)
        pm, pp = main[:, sl], part[:, sl]
        ms = _dot((pm * pm).astype(BF16), bd)
        r = lax.rsqrt(ms + NORM_EPS)
        if s < nq_slabs:
            gm, gp, out_ref, osl = gq_ref[:, sl], gqp_ref[:, sl], gq_o, sl
        else:
            osl = slice((s - nq_slabs) * LANE, (s - nq_slabs + 1) * LANE)
            gm, gp, out_ref = gk_ref[:, osl], gkp_ref[:, osl], gk_o
        out_ref[0, :, osl] = ((pm * r * gm) * cos + (pp * r * gp) * sin).astype(BF16)

    gv_nq = proj(C_GV, GQA_KV_W + NA_W)
    gv = gv_nq[:, :GQA_KV_W]
    gv_o[0] = gv.astype(BF16)
    nq_o[0] = (gv_nq[:, GQA_KV_W:] * (HEAD_DIM ** -0.5 * math.log2(math.e))).astype(BF16)
    nk_nv = proj(C_NK, 2 * NA_W)
    nk_o[0] = nk_nv[:, :NA_W].astype(BF16)
    nv_o[0] = nk_nv[:, NA_W:].astype(BF16)
    lane = lax.broadcasted_iota(jnp.int32, gv.shape, 1)
    for h, vh in enumerate((gv, pltpu.roll(gv, HEAD_DIM, 1))):
        vt = jnp.where(lane < HEAD_DIM, vh, jnp.where(lane == HEAD_DIM, 1.0, 0.0)).T
        gvt_o[0, h] = vt[:GQA_VT_ROWS].astype(BF16)
    gk = gk_o[0].astype(F32)
    gk2_o[0, 0] = jnp.where(lane < HEAD_DIM, gk, jnp.where(lane == HEAD_DIM, 1.0, 0.0)).astype(BF16)
    gk2_o[0, 1] = jnp.where(lane >= HEAD_DIM, gk, jnp.where(lane == 0, 1.0, 0.0)).astype(BF16)


def _inproj(layer, x, shift, scale, w_in, w_rot, lb_logits, cos, sin, gains, bd128, tm):
    b, t, d = x.shape
    gq, gqp, gk, gkp = gains
    row = lambda bi, i: (bi, i, 0)
    const2 = lambda bi, i: (0, 0)
    halves = (HG_W // LANE, 2 * HG_W // LANE, HG_W // LANE)
    widths = (HG_W, GQA_W, GQA_KV_W, GQA_KV_W, NA_W, NA_W, NA_W)
    dtypes = (F32, BF16, BF16, BF16, BF16, BF16, BF16)
    return pl.pallas_call(
        functools.partial(_inproj_kernel, layer),
        grid=(b, t // tm),
        in_specs=[
            pl.BlockSpec((1, tm, d), row),
            pl.BlockSpec((1, 1, d), lambda bi, i: (bi, 0, 0)),
            pl.BlockSpec((1, 1, d), lambda bi, i: (bi, 0, 0)),
            pl.BlockSpec((None, d, w_in.shape[2]), lambda bi, i: (layer, 0, 0)),
            pl.BlockSpec((None, d, w_rot.shape[2]), lambda bi, i: (layer, 0, 0)),
            pl.BlockSpec((DEPTH, 2, HG_W), lambda bi, i: (0, 0, 0)),
            pl.BlockSpec((tm, LANE), lambda bi, i: (i, 0)),
            pl.BlockSpec((tm, LANE), lambda bi, i: (i, 0)),
            pl.BlockSpec((1, GQA_W), const2),
            pl.BlockSpec((1, GQA_W), const2),
            pl.BlockSpec((1, GQA_KV_W), const2),
            pl.BlockSpec((1, GQA_KV_W), const2),
            pl.BlockSpec((LANE, LANE), const2),
        ],
        out_specs=[pl.BlockSpec((1, nh, tm, LANE), lambda bi, i: (bi, 0, i, 0)) for nh in halves]
        + [pl.BlockSpec((1, tm, w), row) for w in widths]
        + [pl.BlockSpec((1, 2, GQA_VT_ROWS, tm), lambda bi, i: (bi, 0, 0, i)),
           pl.BlockSpec((1, 2, tm, LANE), lambda bi, i: (bi, 0, i, 0))],
        out_shape=[jax.ShapeDtypeStruct((b, nh, t, LANE), F32) for nh in halves]
        + [jax.ShapeDtypeStruct((b, t, w), dt) for w, dt in zip(widths, dtypes)]
        + [jax.ShapeDtypeStruct((b, 2, GQA_VT_ROWS, t), BF16), jax.ShapeDtypeStruct((b, 2, t, LANE), BF16)],
        compiler_params=_cparams(("arbitrary", "arbitrary")),
        name="inproj",
    )(x, shift, scale, w_in, w_rot, lb_logits, cos, sin, gq, gqp, gk, gkp, bd128)


def _hgrn_kernel(tb, qf_ref, lff_ref, vf_ref, qb_ref, lfb_ref, vb_ref, s0_ref, ind_ref, bm_ref,
                 of_ref, ob_ref, sfin_ref, st_scr, stb_scr, qs_scr, ks_scr):
    i = pl.program_id(1)
    nsc = tb // SUB

    @pl.when(i == 0)
    def _():
        st_scr[...] = s0_ref[0]
        stb_scr[...] = s0_ref[0].astype(BF16)

    ind = ind_ref[...]
    in_refs = ((qf_ref, lff_ref, vf_ref), (qb_ref, lfb_ref, vb_ref))
    o_refs = (of_ref, ob_ref)

    nh = HG_W // LANE

    def slab(tt):
        return pl.ds(tt, nsc, stride=SUB)

    def load(ref, lead, rows):
        return jnp.concatenate([ref[(*lead, c, rows, slice(None))] for c in range(nh)], axis=1)

    def store(ref, lead, rows, a, add=False):
        for c in range(nh):
            idx = (*lead, c, rows, slice(None))
            piece = a[:, c * LANE:(c + 1) * LANE]
            ref[idx] = ref[idx] + piece if add else piece

    orders = (list(range(SUB)), list(range(SUB - 1, -1, -1)))
    q, v, f, k, dec = [], [], [], [], []
    for d, (q_ref, lf_ref, v_ref) in enumerate(in_refs):
        lf = {tt: load(lf_ref, (0,), slab(tt)) for tt in orders[d]}
        q.append({tt: load(q_ref, (0,), slab(tt)) for tt in orders[d]})
        v.append({tt: load(v_ref, (0,), slab(tt)) for tt in orders[d]})
        f.append({tt: jnp.exp(lf[tt]) for tt in orders[d]})
        k.append({tt: 1.0 - f[d][tt] for tt in orders[d]})
        cum, run = {}, None
        for tt in orders[d]:
            run = lf[tt] if run is None else run + lf[tt]
            cum[tt] = run
        for tt in orders[d]:
            store(qs_scr, (d,), slab(tt), q[d][tt] * jnp.exp(cum[tt]))
            store(ks_scr, (d,), slab(tt), k[d][tt] * jnp.exp(run - cum[tt]))
        dec.append(jnp.exp(run))

    acc = [dict(), dict()]
    for a in range(SUB):
        for d in range(2):
            ss = orders[d][a]
            u = k[d][ss]
            es = []
            for tt in orders[d][a:]:
                if tt != ss:
                    u = u * f[d][tt]
                es.append((q[d][tt] * u).astype(BF16))
            att = _dot(jnp.concatenate(es, axis=0), ind)
            for n, tt in enumerate(orders[d][a:]):
                part = att[n * nsc:(n + 1) * nsc] * v[d][ss]
                acc[d][tt] = part if a == 0 else acc[d][tt] + part
    for d in range(2):
        for tt in orders[d]:
            store(o_refs[d], (0,), slab(tt), acc[d][tt])

    bm = bm_ref[...]
    for step in range(nsc):
        for d in range(2):
            j = nsc - 1 - step if d else step
            sl = slice(j * SUB, (j + 1) * SUB)
            store(o_refs[d], (0,), sl, _dot_nt(load(qs_scr, (d,), sl).astype(BF16), stb_scr[d]), add=True)
            kvt = lax.dot_general(load(in_refs[d][2], (0,), sl).astype(BF16), load(ks_scr, (d,), sl).astype(BF16),
                                  (((0,), (0,)), ((), ())), preferred_element_type=F32)
            for h in range(HG_HEADS):
                rs = slice(h * HEAD_DIM, (h + 1) * HEAD_DIM)
                ls = slice((h // 2) * LANE, (h // 2 + 1) * LANE)
                blk = st_scr[d, rs, ls] * dec[d][j:j + 1, ls] + kvt[rs, ls] * bm[rs, ls]
                st_scr[d, rs, ls] = blk
                stb_scr[d, rs, ls] = blk.astype(BF16)

    @pl.when(i == pl.num_programs(1) - 1)
    def _():
        sfin_ref[0] = st_scr[...]


def _hgrn(q, lf2, v, s0, consts, tb):
    b, nh, t, _ = q.shape
    nblk = t // tb
    ind, bm = consts
    fwd = lambda bi, i: (bi, 0, i, 0)
    bwd = lambda bi, i: (bi, 0, nblk - 1 - i, 0)
    bwd_lf = lambda bi, i: (bi, 1, nblk - 1 - i, 0)
    const2 = lambda bi, i: (0, 0)
    state = lambda bi, i: (bi, 0, 0, 0)
    blk = (1, nh, tb, LANE)
    return pl.pallas_call(
        functools.partial(_hgrn_kernel, tb),
        grid=(b, nblk),
        in_specs=[
            pl.BlockSpec(blk, fwd), pl.BlockSpec(blk, fwd), pl.BlockSpec(blk, fwd),
            pl.BlockSpec(blk, bwd), pl.BlockSpec(blk, bwd_lf), pl.BlockSpec(blk, bwd),
            pl.BlockSpec((1, 2, HG_W, HG_W), state),
            pl.BlockSpec((HG_W, HG_W), const2),
            pl.BlockSpec((HG_W, HG_W), const2),
        ],
        out_specs=[pl.BlockSpec(blk, fwd), pl.BlockSpec(blk, bwd), pl.BlockSpec((1, 2, HG_W, HG_W), state)],
        out_shape=[
            jax.ShapeDtypeStruct((b, nh, t, LANE), F32),
            jax.ShapeDtypeStruct((b, nh, t, LANE), F32),
            jax.ShapeDtypeStruct((b, 2, HG_W, HG_W), F32),
        ],
        scratch_shapes=[pltpu.VMEM((2, HG_W, HG_W), F32), pltpu.VMEM((2, HG_W, HG_W), BF16),
                        pltpu.VMEM((2, nh, tb, LANE), F32), pltpu.VMEM((2, nh, tb, LANE), F32)],
        compiler_params=_cparams(("arbitrary", "arbitrary")),
        name="hgrn",
    )(q, lf2, v, q, lf2, v, s0, ind, bm)


def _hgrn_consts():
    c = np.arange(HG_W)
    blk = (c[:, None] // HEAD_DIM) == (c[None, :] // HEAD_DIM)
    return jnp.asarray(blk.astype(np.float32), BF16), jnp.asarray(blk.astype(np.float32))


def _half_masks(rows):
    lane = lax.broadcasted_iota(jnp.int32, (rows, LANE), 1)
    return lane < HEAD_DIM, lane >= HEAD_DIM


def _sel(mask, a):
    return jnp.where(mask, a, jnp.zeros_like(a))


def _value_slabs(v):
    lane = lax.broadcasted_iota(jnp.int32, v.shape, 1)
    lo, hi = _half_masks(v.shape[0])
    e_lo = jnp.where(lane == HEAD_DIM, 1.0, 0.0).astype(v.dtype)
    e_hi = jnp.where(lane == 0, 1.0, 0.0).astype(v.dtype)
    return _sel(lo, v) + e_lo, _sel(hi, v) + e_hi


def _normalize_slab(a_lo, a_hi):
    lane = lax.broadcasted_iota(jnp.int32, a_lo.shape, 1)
    return jnp.where(lane < HEAD_DIM, a_lo / a_lo[:, HEAD_DIM:HEAD_DIM + 1], a_hi / a_hi[:, 0:1])


def _gqa_kernel(tq, tk, n_lat, q_ref, kl_ref, vtl_ref, kc_ref, vtc_ref, o_ref, kmax_scr, m_scr, acc_scr):
    i = pl.program_id(1)
    nslab = GQA_W // LANE
    n_chunks = n_lat // tk
    heads = [(s, half) for s in range(nslab) for half in range(2)]

    def key_chunk(c):
        start = pl.multiple_of(c * tk, tk)
        return ([kl_ref[0, h, pl.ds(start, tk), :] for h in range(2)],
                [vtl_ref[0, h, :, pl.ds(start, tk)] for h in range(2)])

    @pl.when(i == 0)
    def _():
        for h in range(2):
            def norm2(kb):
                x = kb.astype(F32)
                lane = lax.broadcasted_iota(jnp.int32, x.shape, 1)
                x = jnp.where((lane < HEAD_DIM) if h == 0 else (lane >= HEAD_DIM), x, 0.0)
                mx = jnp.max(jnp.sum(x * x, axis=1, keepdims=True), axis=0, keepdims=True)
                return jnp.broadcast_to(mx, kmax_scr.shape[1:])

            mx = lax.fori_loop(0, n_chunks, lambda c, m: jnp.maximum(m, norm2(key_chunk(c)[0][h])),
                               norm2(kc_ref[0, h]))
            kmax_scr[h] = jnp.sqrt(mx)

    qlo, qhi = _half_masks(tq)
    lane = lax.broadcasted_iota(jnp.int32, (tq, LANE), 1)
    q_all = q_ref[0].astype(F32)
    rr = lax.broadcasted_iota(jnp.int32, (GQA_W, GQA_W), 0) // HEAD_DIM
    cc = lax.broadcasted_iota(jnp.int32, (GQA_W, GQA_W), 1) // HEAD_DIM
    qn2 = _dot((q_all * q_all).astype(BF16), jnp.where(rr == cc, 1.0, 0.0).astype(BF16))
    kmax = jnp.where(lane[0:1] < HEAD_DIM, kmax_scr[0][0:1], kmax_scr[1][0:1])
    bound = jnp.sqrt(qn2) * jnp.concatenate([kmax] * nslab, axis=1)
    use_bound = 2.0 * jnp.max(bound) < GQA_SHIFT_LIMIT

    def query(idx, bound_path):
        s, half = heads[idx]
        qf = q_all[:, s * LANE:(s + 1) * LANE]
        if not bound_path:
            return jnp.where(qhi if half else qlo, qf, 0.0).astype(BF16)
        swapped = pltpu.roll(bound[:, s * LANE:(s + 1) * LANE], HEAD_DIM, 1)
        return jnp.where(lane == (0 if half else HEAD_DIM), -swapped, qf).astype(BF16)

    qshift = [query(idx, True) for idx in range(len(heads))]

    def step_bound(k2, vt):
        sts = [_dot_nt(k2[half], qshift[idx]) for idx, (s, half) in enumerate(heads)]
        for idx, (s, half) in enumerate(heads):
            acc_scr[idx] = acc_scr[idx] + _dot(vt[half], jnp.exp2(sts[idx]).astype(BF16))

    def step_exact(qh, k2, vt):
        sts = [_dot_nt(k2[half], qh[idx]) for idx, (s, half) in enumerate(heads)]
        for idx, (s, half) in enumerate(heads):
            m_old = m_scr[idx]
            m_new = jnp.maximum(m_old, jnp.max(sts[idx], axis=0, keepdims=True))
            pt = jnp.exp2(sts[idx] - m_new).astype(BF16)
            acc_scr[idx] = jnp.exp2(m_old - m_new) * acc_scr[idx] + _dot(vt[half], pt)
            m_scr[idx] = m_new

    def run(step):
        def body(c, carry):
            step(*key_chunk(c))
            return carry

        lax.fori_loop(0, n_chunks, body, 0)
        step([kc_ref[0, h] for h in range(2)], [vtc_ref[0, h] for h in range(2)])

    acc_scr[...] = jnp.zeros(acc_scr.shape, F32)

    @pl.when(use_bound)
    def _():
        run(step_bound)

    @pl.when(jnp.logical_not(use_bound))
    def _():
        m_scr[...] = jnp.full(m_scr.shape, -jnp.inf, F32)
        run(functools.partial(step_exact, [query(idx, False) for idx in range(len(heads))]))

    for s in range(nslab):
        halves = [acc_scr[2 * s + half] for half in range(2)]
        out_t = jnp.concatenate([a[:HEAD_DIM] / a[HEAD_DIM:HEAD_DIM + 1] for a in halves], axis=0)
        o_ref[0, :, s * LANE:(s + 1) * LANE] = out_t.T.astype(BF16)


def _gqa(q, k_lat, vt_lat, k_ctx, vt_ctx, tq, tk):
    b, t, _ = q.shape
    n_lat = k_lat.shape[2]
    n_ctx = k_ctx.shape[2]
    full4 = lambda bi, i: (bi, 0, 0, 0)
    return pl.pallas_call(
        functools.partial(_gqa_kernel, tq, tk, n_lat),
        grid=(b, t // tq),
        in_specs=[
            pl.BlockSpec((1, tq, GQA_W), lambda bi, i: (bi, i, 0)),
            pl.BlockSpec((1, 2, n_lat, LANE), full4),
            pl.BlockSpec((1, 2, GQA_VT_ROWS, n_lat), full4),
            pl.BlockSpec((1, 2, n_ctx, LANE), full4),
            pl.BlockSpec((1, 2, GQA_VT_ROWS, n_ctx), full4),
        ],
        out_specs=pl.BlockSpec((1, tq, GQA_W), lambda bi, i: (bi, i, 0)),
        out_shape=jax.ShapeDtypeStruct((b, t, GQA_W), BF16),
        scratch_shapes=[
            pltpu.VMEM((GQA_KV_HEADS, 8, LANE), F32),
            pltpu.VMEM((GQA_HEADS, 1, tq), F32),
            pltpu.VMEM((GQA_HEADS, GQA_VT_ROWS, tq), F32),
        ],
        compiler_params=_cparams(("arbitrary", "arbitrary")),
        name="gqa",
    )(q, k_lat, vt_lat, k_ctx, vt_ctx)


def _ctxattn_kernel(nslab, shared_kv, q_ref, k_ref, v_ref, o_ref):
    n = q_ref.shape[1]
    lo, hi = _half_masks(n)
    for s in range(nslab):
        ks = 0 if shared_kv else s
        q = q_ref[0, :, s * LANE:(s + 1) * LANE]
        k = k_ref[0, :, ks * LANE:(ks + 1) * LANE]
        v = v_ref[0, :, ks * LANE:(ks + 1) * LANE]
        out = None
        for mask in (lo, hi):
            sc = _dot_nt(_sel(mask, q), k)
            p = jnp.exp2(sc - jnp.max(sc, axis=1, keepdims=True))
            o_h = _dot(p.astype(BF16), _sel(mask, v)) / jnp.sum(p, axis=1, keepdims=True)
            out = o_h if out is None else out + o_h
        o_ref[0, :, s * LANE:(s + 1) * LANE] = out.astype(BF16)


def _ctxattn(q, k, v, shared_kv):
    b, n, w = q.shape
    kw = k.shape[2]
    full = lambda bi: (bi, 0, 0)
    return pl.pallas_call(
        functools.partial(_ctxattn_kernel, w // LANE, shared_kv),
        grid=(b,),
        in_specs=[pl.BlockSpec((1, n, w), full), pl.BlockSpec((1, n, kw), full), pl.BlockSpec((1, n, kw), full)],
        out_specs=pl.BlockSpec((1, n, w), full),
        out_shape=jax.ShapeDtypeStruct((b, n, w), BF16),
        compiler_params=_cparams(("arbitrary",)),
        name="ctxattn",
    )(q, k, v)


def _na_window_start(i, r, rows):
    return jnp.clip(i * r - NA_KH // 2, 0, rows - (r + NA_KH))


def _na_kernel(r, rows, q_ref, k_ref, v_ref, kc_ref, vc_ref, eb_ref, br_ref, o_ref, kmax_scr):
    i = pl.program_id(1)
    tq = r * GRID_W
    kwin = (r + NA_KH) * GRID_W
    n_lat = k_ref.shape[1]
    n_ctx = kc_ref.shape[1]
    heads = [(s, half) for s in range(NA_W // LANE) for half in range(2)]

    @pl.when(i == 0)
    def _():
        rr = lax.broadcasted_iota(jnp.int32, (NA_W, NA_W), 0) // HEAD_DIM
        cc = lax.broadcasted_iota(jnp.int32, (NA_W, NA_W), 1) // HEAD_DIM
        ones_bd = jnp.where(rr == cc, 1.0, 0.0).astype(BF16)

        def norm2(kb):
            x = kb.astype(F32)
            n2 = _dot((x * x).astype(BF16), ones_bd)
            return jnp.broadcast_to(jnp.max(n2, axis=0, keepdims=True), kmax_scr.shape)

        ck = min(1024, n_lat)
        mx = lax.fori_loop(0, n_lat // ck,
                           lambda c, m: jnp.maximum(m, norm2(k_ref[0, pl.ds(pl.multiple_of(c * ck, ck), ck), :])),
                           norm2(kc_ref[0]))
        kmax_scr[...] = jnp.sqrt(mx)

    start = pl.multiple_of(_na_window_start(i, r, rows) * GRID_W, GRID_W)
    lane = lax.broadcasted_iota(jnp.int32, (tq, LANE), 1)
    qlo, qhi = _half_masks(tq)

    def ones_lane(n, half, dtype):
        ln = lax.broadcasted_iota(jnp.int32, (n, LANE), 1)
        return jnp.where(ln == (0 if half else HEAD_DIM), 1.0, 0.0).astype(dtype)

    q_all = q_ref[0].astype(F32)
    rr = lax.broadcasted_iota(jnp.int32, (NA_W, NA_W), 0) // HEAD_DIM
    cc = lax.broadcasted_iota(jnp.int32, (NA_W, NA_W), 1) // HEAD_DIM
    qn2 = _dot((q_all * q_all).astype(BF16), jnp.where(rr == cc, 1.0, 0.0).astype(BF16))
    reach = jnp.sqrt(qn2) * kmax_scr[0:1, :]
    shift = reach + br_ref[0][0:1, :]
    use_bound = jnp.max(2.0 * reach + (br_ref[0][0:1, :] - br_ref[1][0:1, :])) < GQA_SHIFT_LIMIT
    def query(h, bound_path):
        s, half = heads[h]
        qf = q_all[:, s * LANE:(s + 1) * LANE]
        if not bound_path:
            return jnp.where(qhi if half else qlo, qf, 0.0).astype(BF16)
        swapped = pltpu.roll(shift[:, s * LANE:(s + 1) * LANE], HEAD_DIM, 1)
        return jnp.where(lane == (0 if half else HEAD_DIM), -swapped, qf).astype(BF16)

    def attend(bound_path):
        for s in range(NA_W // LANE):
            sl = slice(s * LANE, (s + 1) * LANE)
            kw = k_ref[0, pl.ds(start, kwin), sl]
            kc = kc_ref[0, :, sl]
            vws = _value_slabs(v_ref[0, pl.ds(start, kwin), sl])
            vcs = _value_slabs(vc_ref[0, :, sl])
            wm, cm = _half_masks(kwin), _half_masks(n_ctx)
            outs = []
            for half in range(2):
                h = 2 * s + half
                qv = query(h, bound_path)
                if bound_path:
                    s_loc = _dot_nt(qv, _sel(wm[half], kw) + ones_lane(kwin, half, BF16)) + eb_ref[0, h]
                    s_ctx = _dot_nt(qv, _sel(cm[half], kc) + ones_lane(n_ctx, half, BF16))
                else:
                    s_loc = _dot_nt(qv, kw) + eb_ref[0, h]
                    s_ctx = _dot_nt(qv, kc)
                    m = jnp.maximum(jnp.max(s_loc, axis=1, keepdims=True), jnp.max(s_ctx, axis=1, keepdims=True))
                    s_loc, s_ctx = s_loc - m, s_ctx - m
                outs.append(_dot(jnp.exp2(s_loc).astype(BF16), vws[half])
                            + _dot(jnp.exp2(s_ctx).astype(BF16), vcs[half]))
            o_ref[0, :, sl] = _normalize_slab(outs[0], outs[1]).astype(BF16)

    @pl.when(use_bound)
    def _():
        attend(True)

    @pl.when(jnp.logical_not(use_bound))
    def _():
        attend(False)


def _na(layer, q, k, v, k_ctx, v_ctx, eb, brange, r):
    b, t, w = q.shape
    rows = t // GRID_W
    nblk = rows // r
    tq = r * GRID_W
    kwin = (r + NA_KH) * GRID_W
    n_ctx = k_ctx.shape[1]
    full = lambda bi, i: (bi, 0, 0)

    def variant(bi, i):
        return (jnp.where(i == 0, 0, jnp.where(i == nblk - 1, 2, 1)), layer, 0, 0)

    return pl.pallas_call(
        functools.partial(_na_kernel, r, rows),
        grid=(b, nblk),
        in_specs=[
            pl.BlockSpec((1, tq, w), lambda bi, i: (bi, i, 0)),
            pl.BlockSpec((1, t, w), full),
            pl.BlockSpec((1, t, w), full),
            pl.BlockSpec((1, n_ctx, w), full),
            pl.BlockSpec((1, n_ctx, w), full),
            pl.BlockSpec((1, NA_HEADS, tq, kwin), variant),
            pl.BlockSpec((2, 8, NA_W), lambda bi, i: (0, 0, 0)),
        ],
        out_specs=pl.BlockSpec((1, tq, w), lambda bi, i: (bi, i, 0)),
        out_shape=jax.ShapeDtypeStruct((b, t, w), BF16),
        scratch_shapes=[pltpu.VMEM((8, NA_W), F32)],
        compiler_params=_cparams(("arbitrary", "arbitrary")),
        name="na",
    )(q, k, v, k_ctx, v_ctx, eb, brange)


def _na_bias_tables(rpb, r, rows):
    nblk = rows // r
    nkr = r + NA_KH
    qc = np.arange(GRID_W)
    kc = np.arange(GRID_W)
    cs = np.clip(qc - NA_KW // 2, 0, GRID_W - NA_KW)
    col_ok = (kc[None, :] >= cs[:, None]) & (kc[None, :] < cs[:, None] + NA_KW)
    dc = kc[None, :] - qc[:, None] + NA_KW - 1
    col_oh = (dc[None] == np.arange(2 * NA_KW - 1)[:, None, None]) & col_ok[None]
    t1 = jnp.einsum("hij,jqk->hiqk", rpb, jnp.asarray(col_oh.astype(np.float32)), precision=lax.Precision.HIGHEST)
    t1 = t1 + jnp.asarray(np.where(col_ok, 0.0, NEG).astype(np.float32))
    neg_blk = jnp.full((rpb.shape[0], GRID_W, GRID_W), NEG, F32)
    variants = []
    for blk in (0, min(1, nblk - 1), nblk - 1):
        r0 = blk * r
        lo = int(np.clip(r0 - NA_KH // 2, 0, rows - nkr))
        rows_out = []
        for a in range(r):
            qr = r0 + a
            rs = int(np.clip(qr - NA_KH // 2, 0, rows - NA_KH))
            blks = []
            for bk in range(nkr):
                kr = lo + bk
                blks.append(t1[:, kr - qr + NA_KH - 1] if rs <= kr < rs + NA_KH else neg_blk)
            rows_out.append(jnp.concatenate(blks, axis=-1))
        variants.append(jnp.concatenate(rows_out, axis=1))
    return jnp.stack(variants)


def _layer_norm(x, g, b):
    xc = x - jnp.mean(x, axis=-1, keepdims=True)
    return xc * lax.rsqrt(jnp.mean(xc * xc, axis=-1, keepdims=True) + NORM_EPS) * g + b


def _post_kernel(hc, x_ref, of_ref, ob_ref, sg_ref, gq_ref, na_ref, mod_ref, hn_ref, bd_ref, wo_ref, w1_ref, w2_ref,
                 ln_ref, o_ref):
    g1, sh2, sc2, g2 = (mod_ref[0, k:k + 1, :] for k in range(4))
    tm = x_ref.shape[1]
    parts = [slice(p * (tm // POST_PARTS), (p + 1) * (tm // POST_PARTS)) for p in range(POST_PARTS)]
    mix = []
    for rs in parts:
        o = jnp.concatenate([of_ref[0, c, rs, :] + ob_ref[0, c, rs, :] for c in range(HG_W // LANE)], axis=1)
        ms = _dot((o * o).astype(BF16), bd_ref[...])
        hg = (o * lax.rsqrt(ms + NORM_EPS) * hn_ref[...] * sg_ref[0, rs, :]).astype(BF16)
        mix.append(jnp.concatenate([hg, gq_ref[0, rs, :], na_ref[0, rs, :]], axis=1))
    ys = [_dot(m, wo_ref[...]) for m in mix]
    x1s = [_layer_norm(DN_ALPHA * x_ref[0, rs, :] + g1 * y, ln_ref[0:1, :], ln_ref[1:2, :]) for rs, y in zip(parts, ys)]
    h2s = [(x1 * (1.0 + sc2) + sh2).astype(BF16) for x1 in x1s]
    fs, c0 = [None] * POST_PARTS, 0
    for width in hc:
        ugs = [_dot(h2, w1_ref[:, c0:c0 + width]) for h2 in h2s]
        uus = [_dot(h2, w1_ref[:, FFN_HIDDEN + c0:FFN_HIDDEN + c0 + width]) for h2 in h2s]
        for p in range(POST_PARTS):
            a = (ugs[p] * _sigmoid(ugs[p]) * uus[p]).astype(BF16)
            part = _dot(a, w2_ref[c0:c0 + width, :])
            fs[p] = part if fs[p] is None else fs[p] + part
        c0 += width
    for rs, x1, f in zip(parts, x1s, fs):
        o_ref[0, rs, :] = _layer_norm(DN_ALPHA * x1 + g2 * f, ln_ref[2:3, :], ln_ref[3:4, :])


def _post(layer, x, o_f, o_b, sg, gq, na, mod4, hn, bd256, w_out, w1, w2, ln4, tm, hc):
    b, t, d = x.shape
    row = lambda bi, i: (bi, i, 0)
    const2 = lambda bi, i: (0, 0)
    per_layer = lambda bi, i: (layer, 0, 0)
    return pl.pallas_call(
        functools.partial(_post_kernel, hc),
        grid=(b, t // tm),
        in_specs=[
            pl.BlockSpec((1, tm, d), row),
            pl.BlockSpec((1, HG_W // LANE, tm, LANE), lambda bi, i: (bi, 0, i, 0)),
            pl.BlockSpec((1, HG_W // LANE, tm, LANE), lambda bi, i: (bi, 0, i, 0)),
            pl.BlockSpec((1, tm, HG_W), row),
            pl.BlockSpec((1, tm, GQA_W), row),
            pl.BlockSpec((1, tm, NA_W), row),
            pl.BlockSpec((1, 4, d), lambda bi, i: (bi, 0, 0)),
            pl.BlockSpec((1, HG_W), const2),
            pl.BlockSpec((HG_W, HG_W), const2),
            pl.BlockSpec((None, d, d), per_layer),
            pl.BlockSpec((None, d, 2 * FFN_HIDDEN), per_layer),
            pl.BlockSpec((None, FFN_HIDDEN, d), per_layer),
            pl.BlockSpec((4, d), const2),
        ],
        out_specs=pl.BlockSpec((1, tm, d), row),
        out_shape=jax.ShapeDtypeStruct((b, t, d), F32),
        compiler_params=_cparams(("arbitrary", "arbitrary")),
        name="post",
    )(x, o_f, o_b, sg, gq, na, mod4, hn, bd256, w_out, w1, w2, ln4)


_PERM_MAIN = np.concatenate([np.arange(0, HEAD_DIM, 2), np.arange(1, HEAD_DIM, 2)])
_PERM_PART = np.concatenate([np.arange(1, HEAD_DIM, 2), np.arange(0, HEAD_DIM, 2)])


def _w_in_split(w_in):
    w = w_in.astype(BF16)
    gq0 = 5 * HG_W
    gk0 = gq0 + GQA_W

    def head(base, h, odd_first):
        pairs = w[:, :, base + h * HEAD_DIM:base + (h + 1) * HEAD_DIM].reshape(w.shape[0], w.shape[1], HEAD_DIM // 2, 2)
        even, odd = pairs[..., 0], pairs[..., 1]
        return jnp.concatenate([odd, even] if odd_first else [even, odd], axis=-1)

    cols = [head(gq0, h, False) for h in GQA_SLAB_HEADS] + [head(gk0, h, False) for h in range(GQA_KV_HEADS)]
    cols += [head(gq0, h, True) for h in GQA_SLAB_HEADS] + [head(gk0, h, True) for h in range(GQA_KV_HEADS)]
    return w, jnp.concatenate(cols, axis=-1)


def _w_out_rows(w_out):
    g0 = HG_W
    blocks = [w_out[:, :g0]] + [w_out[:, g0 + h * HEAD_DIM:g0 + (h + 1) * HEAD_DIM] for h in GQA_SLAB_HEADS]
    return jnp.concatenate(blocks + [w_out[:, g0 + GQA_W:]], axis=1).astype(BF16)


def _rope_tables(n_tokens):
    t = np.arange(n_tokens)
    row = (t // GRID_W).astype(np.float32)
    col = (t % GRID_W).astype(np.float32)
    n_pairs = HEAD_DIM // 4
    inv_freq = np.exp(-math.log(ROPE_THETA) * np.arange(n_pairs, dtype=np.float32) / n_pairs).astype(np.float32)
    ang = np.concatenate([row[:, None] * inv_freq, col[:, None] * inv_freq], -1).astype(np.float32)
    ang = jnp.asarray(ang)
    c, s = jnp.cos(ang), jnp.sin(ang)
    cos = jnp.tile(jnp.concatenate([c, c], -1), (1, LANE // HEAD_DIM))
    sin = jnp.tile(jnp.concatenate([-s, s], -1), (1, LANE // HEAD_DIM))
    return cos, sin


def _block_diag_mean(width):
    c = np.arange(width)
    return jnp.asarray(((c[:, None] // HEAD_DIM) == (c[None, :] // HEAD_DIM)).astype(np.float32) / HEAD_DIM, BF16)


def kernel(x, c, ctx, c_ctx, w_ada, b_ada, w_in, hgrn_lb_logits, hgrn_norm, gqa_q_norm, gqa_k_norm, na_rpb, w_out,
           w_ffn_in, w_ffn_out, ln_g, ln_b):
    b, n, d = x.shape
    n_ctx = ctx.shape[1]
    rows = n // GRID_W
    tm = min(512, n)
    tm_ctx = min(256, n_ctx)
    tb = 256
    tq, tk = min(512, n), min(2048, n)
    na_r = 4
    hc = (1536, FFN_HIDDEN - 1536)

    cv = jnp.zeros((MOD_ROWS, d), F32).at[:b].set(c).at[b].set(c_ctx)
    mods = _mods(cv, w_ada, b_ada)

    w_in_b, w_rot = _w_in_split(w_in)
    w_out_p = _w_out_rows(w_out)
    w1 = w_ffn_in.astype(BF16)
    w2 = w_ffn_out.astype(BF16)

    cos_l, sin_l = _rope_tables(n)
    cos_c = jnp.ones((n_ctx, LANE), F32)
    sin_c = jnp.zeros((n_ctx, LANE), F32)
    bd128 = _block_diag_mean(LANE)
    bd256 = _block_diag_mean(HG_W)
    hg_consts = _hgrn_consts()
    s_zero = jnp.zeros((b, 2, HG_W, HG_W), F32)

    rpb2_all = na_rpb * math.log2(math.e)
    eb_all = _na_bias_tables(rpb2_all.reshape((-1,) + na_rpb.shape[2:]), na_r, rows)

    x_lat, x_ctx = x, ctx
    for l in range(DEPTH):
        need_ctx = l < DEPTH - 1
        m = mods[l].reshape(MOD_ROWS, N_MOD, d)
        m_lat = m[:b]
        m_ctx = jnp.broadcast_to(m[b][None], (b, N_MOD, d))
        qscale = HEAD_DIM ** -0.5 * math.log2(math.e)
        gains = (
            (jnp.tile(gqa_q_norm[l][_PERM_MAIN], GQA_HEADS) * qscale)[None],
            (jnp.tile(gqa_q_norm[l][_PERM_PART], GQA_HEADS) * qscale)[None],
            jnp.tile(gqa_k_norm[l][_PERM_MAIN], GQA_KV_HEADS)[None],
            jnp.tile(gqa_k_norm[l][_PERM_PART], GQA_KV_HEADS)[None],
        )
        p_lat = _inproj(l, x_lat, m_lat[:, 0:1], m_lat[:, 1:2], w_in_b, w_rot, hgrn_lb_logits, cos_l, sin_l, gains, bd128, tm)
        p_ctx = _inproj(l, x_ctx, m_ctx[:, 0:1], m_ctx[:, 1:2], w_in_b, w_rot, hgrn_lb_logits, cos_c, sin_c, gains, bd128,
                        tm_ctx)
        hq_l, lf_l, hi_l, sg_l, gq_l, gk_l, gv_l, nq_l, nk_l, nv_l, gvt_l, gk2_l = p_lat
        hq_c, lf_c, hi_c, sg_c, gq_c, gk_c, gv_c, nq_c, nk_c, nv_c, gvt_c, gk2_c = p_ctx

        *o_c, s_c = _hgrn(hq_c, lf_c, hi_c, s_zero, hg_consts, min(tb, n_ctx))
        *o_l, _ = _hgrn(hq_l, lf_l, hi_l, s_c, hg_consts, tb)

        gqa_l = _gqa(gq_l, gk2_l, gvt_l, gk2_c, gvt_c, tq, tk)
        rpb2 = rpb2_all[l]
        b_hi = jnp.maximum(jnp.max(rpb2, axis=(1, 2)), 0.0)
        b_lo = jnp.minimum(jnp.min(rpb2, axis=(1, 2)), 0.0)
        brange = jnp.broadcast_to(jnp.repeat(jnp.stack([b_hi, b_lo]), HEAD_DIM, axis=1)[:, None, :], (2, 8, NA_W))
        na_l = _na(l, nq_l, nk_l, nv_l, nk_c, nv_c, eb_all, brange, na_r)

        hn = jnp.tile(hgrn_norm[l], HG_HEADS)[None]
        ln4 = jnp.stack([ln_g[l, 0], ln_b[l, 0], ln_g[l, 1], ln_b[l, 1]])
        if need_ctx:
            gqa_c = _ctxattn(gq_c, gk_c, gv_c, True)
            na_c = _ctxattn(nq_c, nk_c, nv_c, False)
            x_ctx = _post(l, x_ctx, o_c[0], o_c[1], sg_c, gqa_c, na_c, m_ctx[:, 2:6], hn, bd256, w_out_p, w1, w2,
                          ln4, tm_ctx, hc)
        x_lat = _post(l, x_lat, o_l[0], o_l[1], sg_l, gqa_l, na_l, m_lat[:, 2:6], hn, bd256, w_out_p, w1, w2,
                      ln4, tm, hc)
    return x_lat
```

```python
import functools
import math

import numpy as np
import jax
import jax.numpy as jnp
from jax import lax
from jax.experimental import pallas as pl
from jax.experimental.pallas import tpu as pltpu

F32 = jnp.float32
BF16 = jnp.bfloat16

D_MODEL = 1024
DEPTH = 2
GRID_W = 64
HEAD_DIM = 64
HG_HEADS = 4
GQA_HEADS = 6
GQA_KV_HEADS = 2
NA_HEADS = 6
NA_KH = 8
NA_KW = 16
ROPE_THETA = 10000.0
NORM_EPS = 1e-6
N_MOD = 6
HG_W = HG_HEADS * HEAD_DIM
GQA_W = GQA_HEADS * HEAD_DIM
GQA_KV_W = GQA_KV_HEADS * HEAD_DIM
NA_W = NA_HEADS * HEAD_DIM
FFN_HIDDEN = 2816
DN_ALPHA = (2 * DEPTH) ** 0.25

LANE = 128
SUB = 16
MOD_ROWS = 8
NEG = -1e30
POST_PARTS = 2
GQA_VT_ROWS = 80
GQA_SHIFT_LIMIT = 100.0
VMEM_LIMIT = 56 * 1024 * 1024

C_HQ, C_FF, C_FB, C_HI, C_HGATE = 0, 256, 512, 768, 1024
C_GV, C_NK = 1792, 2304
GQA_SLAB_HEADS = (0, 3, 1, 4, 2, 5)


def _cparams(sem, flags=None):
    return pltpu.CompilerParams(dimension_semantics=sem, vmem_limit_bytes=VMEM_LIMIT, flags=flags)


def _sigmoid(z):
    return 1.0 / (1.0 + jnp.exp(-z))


def _dot(a, b):
    return jnp.dot(a, b, preferred_element_type=F32)


def _dot_nt(a, b):
    return lax.dot_general(a, b, (((1,), (1,)), ((), ())), preferred_element_type=F32)


def _mods_kernel(cv_ref, w_ref, b_ref, o_ref):
    cv = cv_ref[...]
    a = cv * _sigmoid(cv)
    o_ref[0] = jnp.dot(a, w_ref[0], preferred_element_type=F32, precision=lax.Precision.HIGHEST) + b_ref[0]


def _mods(cv, w_ada, b_ada):
    depth, d, nm = w_ada.shape
    tn = 1024
    return pl.pallas_call(
        _mods_kernel,
        grid=(depth, nm // tn),
        in_specs=[
            pl.BlockSpec((MOD_ROWS, d), lambda l, j: (0, 0)),
            pl.BlockSpec((1, d, tn), lambda l, j: (l, 0, j)),
            pl.BlockSpec((1, 1, tn), lambda l, j: (l, 0, j)),
        ],
        out_specs=pl.BlockSpec((1, MOD_ROWS, tn), lambda l, j: (l, 0, j)),
        out_shape=jax.ShapeDtypeStruct((depth, MOD_ROWS, nm), F32),
        compiler_params=_cparams(("arbitrary", "arbitrary")),
        name="mods",
    )(cv, w_ada, b_ada.reshape(depth, 1, nm))


def _inproj_kernel(layer, x_ref, sh_ref, sc_ref, w_ref, wr_ref, lbl_ref, cos_ref, sin_ref, gq_ref, gqp_ref, gk_ref, gkp_ref,
                   bd_ref, hq_ref, lf_ref, hi_ref, hg_ref, gq_o, gk_o, gv_o, nq_o, nk_o, nv_o, gvt_o, gk2_o):
    x = x_ref[0]
    h = (x * (1.0 + sc_ref[0]) + sh_ref[0]).astype(BF16)

    def proj(c0, width):
        return _dot(h, w_ref[:, c0:c0 + width])

    logits = [lbl_ref[d] for d in range(DEPTH)]
    mx = functools.reduce(jnp.maximum, logits)
    es = [jnp.exp(v - mx) for v in logits]
    inv = 1.0 / functools.reduce(lambda a, b: a + b, es)
    ps = [e * inv for e in es]
    lb = jnp.clip(functools.reduce(lambda a, b: a + b, ps[:layer + 1]) - ps[0], 0.0, 1.0)

    def put_halves(ref, first, a):
        for c in range(HG_W // LANE):
            ref[0, first + c] = a[:, c * LANE:(c + 1) * LANE]

    q = proj(C_HQ, HG_W)
    put_halves(hq_ref, 0, q * _sigmoid(q))
    for d, c0 in enumerate((C_FF, C_FB)):
        z = proj(c0, HG_W)
        lbd = lb[d:d + 1, :]
        put_halves(lf_ref, d * (HG_W // LANE), jnp.log(lbd + (1.0 - lbd) * _sigmoid(z)))
    put_halves(hi_ref, 0, proj(C_HI, HG_W))
    g = proj(C_HGATE, HG_W)
    hg_ref[0] = g * _sigmoid(g)

    cos = cos_ref[...]
    sin = sin_ref[...]
    bd = bd_ref[...]

    rot_w = GQA_W + GQA_KV_W
    main = _dot(h, wr_ref[:, :rot_w])
    part = _dot(h, wr_ref[:, rot_w:])
    nq_slabs = GQA_W // LANE
    for s in range(rot_w // LANE):
        sl = slice(s * LANE, (s + 1) * LANE)
        pm, pp = main[:, sl], part[:, sl]
        ms = _dot((pm * pm).astype(BF16), bd)
        r = lax.rsqrt(ms + NORM_EPS)
        if s < nq_slabs:
            gm, gp, out_ref, osl = gq_ref[:, sl], gqp_ref[:, sl], gq_o, sl
        else:
            osl = slice((s - nq_slabs) * LANE, (s - nq_slabs + 1) * LANE)
            gm, gp, out_ref = gk_ref[:, osl], gkp_ref[:, osl], gk_o
        out_ref[0, :, osl] = ((pm * r * gm) * cos + (pp * r * gp) * sin).astype(BF16)

    gv_nq = proj(C_GV, GQA_KV_W + NA_W)
    gv = gv_nq[:, :GQA_KV_W]
    gv_o[0] = gv.astype(BF16)
    nq_o[0] = (gv_nq[:, GQA_KV_W:] * (HEAD_DIM ** -0.5 * math.log2(math.e))).astype(BF16)
    nk_nv = proj(C_NK, 2 * NA_W)
    nk_o[0] = nk_nv[:, :NA_W].astype(BF16)
    nv_o[0] = nk_nv[:, NA_W:].astype(BF16)
    lane = lax.broadcasted_iota(jnp.int32, gv.shape, 1)
    for h, vh in enumerate((gv, pltpu.roll(gv, HEAD_DIM, 1))):
        vt = jnp.where(lane < HEAD_DIM, vh, jnp.where(lane == HEAD_DIM, 1.0, 0.0)).T
        gvt_o[0, h] = vt[:GQA_VT_ROWS].astype(BF16)
    gk = gk_o[0].astype(F32)
    gk2_o[0, 0] = jnp.where(lane < HEAD_DIM, gk, jnp.where(lane == HEAD_DIM, 1.0, 0.0)).astype(BF16)
    gk2_o[0, 1] = jnp.where(lane >= HEAD_DIM, gk, jnp.where(lane == 0, 1.0, 0.0)).astype(BF16)


def _inproj(layer, x, shift, scale, w_in, w_rot, lb_logits, cos, sin, gains, bd128, tm):
    b, t, d = x.shape
    gq, gqp, gk, gkp = gains
    row = lambda bi, i: (bi, i, 0)
    const2 = lambda bi, i: (0, 0)
    halves = (HG_W // LANE, 2 * HG_W // LANE, HG_W // LANE)
    widths = (HG_W, GQA_W, GQA_KV_W, GQA_KV_W, NA_W, NA_W, NA_W)
    dtypes = (F32, BF16, BF16, BF16, BF16, BF16, BF16)
    return pl.pallas_call(
        functools.partial(_inproj_kernel, layer),
        grid=(b, t // tm),
        in_specs=[
            pl.BlockSpec((1, tm, d), row),
            pl.BlockSpec((1, 1, d), lambda bi, i: (bi, 0, 0)),
            pl.BlockSpec((1, 1, d), lambda bi, i: (bi, 0, 0)),
            pl.BlockSpec((None, d, w_in.shape[2]), lambda bi, i: (layer, 0, 0)),
            pl.BlockSpec((None, d, w_rot.shape[2]), lambda bi, i: (layer, 0, 0)),
            pl.BlockSpec((DEPTH, 2, HG_W), lambda bi, i: (0, 0, 0)),
            pl.BlockSpec((tm, LANE), lambda bi, i: (i, 0)),
            pl.BlockSpec((tm, LANE), lambda bi, i: (i, 0)),
            pl.BlockSpec((1, GQA_W), const2),
            pl.BlockSpec((1, GQA_W), const2),
            pl.BlockSpec((1, GQA_KV_W), const2),
            pl.BlockSpec((1, GQA_KV_W), const2),
            pl.BlockSpec((LANE, LANE), const2),
        ],
        out_specs=[pl.BlockSpec((1, nh, tm, LANE), lambda bi, i: (bi, 0, i, 0)) for nh in halves]
        + [pl.BlockSpec((1, tm, w), row) for w in widths]
        + [pl.BlockSpec((1, 2, GQA_VT_ROWS, tm), lambda bi, i: (bi, 0, 0, i)),
           pl.BlockSpec((1, 2, tm, LANE), lambda bi, i: (bi, 0, i, 0))],
        out_shape=[jax.ShapeDtypeStruct((b, nh, t, LANE), F32) for nh in halves]
        + [jax.ShapeDtypeStruct((b, t, w), dt) for w, dt in zip(widths, dtypes)]
        + [jax.ShapeDtypeStruct((b, 2, GQA_VT_ROWS, t), BF16), jax.ShapeDtypeStruct((b, 2, t, LANE), BF16)],
        compiler_params=_cparams(("arbitrary", "arbitrary")),
        name="inproj",
    )(x, shift, scale, w_in, w_rot, lb_logits, cos, sin, gq, gqp, gk, gkp, bd128)


def _hgrn_kernel(tb, qf_ref, lff_ref, vf_ref, qb_ref, lfb_ref, vb_ref, s0_ref, ind_ref, bm_ref,
                 of_ref, ob_ref, sfin_ref, st_scr, stb_scr, qs_scr, ks_scr):
    i = pl.program_id(1)
    nsc = tb // SUB

    @pl.when(i == 0)
    def _():
        st_scr[...] = s0_ref[0]
        stb_scr[...] = s0_ref[0].astype(BF16)

    ind = ind_ref[...]
    in_refs = ((qf_ref, lff_ref, vf_ref), (qb_ref, lfb_ref, vb_ref))
    o_refs = (of_ref, ob_ref)

    nh = HG_W // LANE

    def slab(tt):
        return pl.ds(tt, nsc, stride=SUB)

    def load(ref, lead, rows):
        return jnp.concatenate([ref[(*lead, c, rows, slice(None))] for c in range(nh)], axis=1)

    def store(ref, lead, rows, a, add=False):
        for c in range(nh):
            idx = (*lead, c, rows, slice(None))
            piece = a[:, c * LANE:(c + 1) * LANE]
            ref[idx] = ref[idx] + piece if add else piece

    orders = (list(range(SUB)), list(range(SUB - 1, -1, -1)))
    q, v, f, k, dec = [], [], [], [], []
    for d, (q_ref, lf_ref, v_ref) in enumerate(in_refs):
        lf = {tt: load(lf_ref, (0,), slab(tt)) for tt in orders[d]}
        q.append({tt: load(q_ref, (0,), slab(tt)) for tt in orders[d]})
        v.append({tt: load(v_ref, (0,), slab(tt)) for tt in orders[d]})
        f.append({tt: jnp.exp(lf[tt]) for tt in orders[d]})
        k.append({tt: 1.0 - f[d][tt] for tt in orders[d]})
        cum, run = {}, None
        for tt in orders[d]:
            run = lf[tt] if run is None else run + lf[tt]
            cum[tt] = run
        for tt in orders[d]:
            store(qs_scr, (d,), slab(tt), q[d][tt] * jnp.exp(cum[tt]))
            store(ks_scr, (d,), slab(tt), k[d][tt] * jnp.exp(run - cum[tt]))
        dec.append(jnp.exp(run))

    acc = [dict(), dict()]
    for a in range(SUB):
        for d in range(2):
            ss = orders[d][a]
            u = k[d][ss]
            es = []
            for tt in orders[d][a:]:
                if tt != ss:
                    u = u * f[d][tt]
                es.append((q[d][tt] * u).astype(BF16))
            att = _dot(jnp.concatenate(es, axis=0), ind)
            for n, tt in enumerate(orders[d][a:]):
                part = att[n * nsc:(n + 1) * nsc] * v[d][ss]
                acc[d][tt] = part if a == 0 else acc[d][tt] + part
    for d in range(2):
        for tt in orders[d]:
            store(o_refs[d], (0,), slab(tt), acc[d][tt])

    bm = bm_ref[...]
    for step in range(nsc):
        for d in range(2):
            j = nsc - 1 - step if d else step
            sl = slice(j * SUB, (j + 1) * SUB)
            store(o_refs[d], (0,), sl, _dot_nt(load(qs_scr, (d,), sl).astype(BF16), stb_scr[d]), add=True)
            kvt = lax.dot_general(load(in_refs[d][2], (0,), sl).astype(BF16), load(ks_scr, (d,), sl).astype(BF16),
                                  (((0,), (0,)), ((), ())), preferred_element_type=F32)
            for h in range(HG_HEADS):
                rs = slice(h * HEAD_DIM, (h + 1) * HEAD_DIM)
                ls = slice((h // 2) * LANE, (h // 2 + 1) * LANE)
                blk = st_scr[d, rs, ls] * dec[d][j:j + 1, ls] + kvt[rs, ls] * bm[rs, ls]
                st_scr[d, rs, ls] = blk
                stb_scr[d, rs, ls] = blk.astype(BF16)

    @pl.when(i == pl.num_programs(1) - 1)
    def _():
        sfin_ref[0] = st_scr[...]


def _hgrn(q, lf2, v, s0, consts, tb):
    b, nh, t, _ = q.shape
    nblk = t // tb
    ind, bm = consts
    fwd = lambda bi, i: (bi, 0, i, 0)
    bwd = lambda bi, i: (bi, 0, nblk - 1 - i, 0)
    bwd_lf = lambda bi, i: (bi, 1, nblk - 1 - i, 0)
    const2 = lambda bi, i: (0, 0)
    state = lambda bi, i: (bi, 0, 0, 0)
    blk = (1, nh, tb, LANE)
    return pl.pallas_call(
        functools.partial(_hgrn_kernel, tb),
        grid=(b, nblk),
        in_specs=[
            pl.BlockSpec(blk, fwd), pl.BlockSpec(blk, fwd), pl.BlockSpec(blk, fwd),
            pl.BlockSpec(blk, bwd), pl.BlockSpec(blk, bwd_lf), pl.BlockSpec(blk, bwd),
            pl.BlockSpec((1, 2, HG_W, HG_W), state),
            pl.BlockSpec((HG_W, HG_W), const2),
            pl.BlockSpec((HG_W, HG_W), const2),
        ],
        out_specs=[pl.BlockSpec(blk, fwd), pl.BlockSpec(blk, bwd), pl.BlockSpec((1, 2, HG_W, HG_W), state)],
        out_shape=[
            jax.ShapeDtypeStruct((b, nh, t, LANE), F32),
            jax.ShapeDtypeStruct((b, nh, t, LANE), F32),
            jax.ShapeDtypeStruct((b, 2, HG_W, HG_W), F32),
        ],
        scratch_shapes=[pltpu.VMEM((2, HG_W, HG_W), F32), pltpu.VMEM((2, HG_W, HG_W), BF16),
                        pltpu.VMEM((2, nh, tb, LANE), F32), pltpu.VMEM((2, nh, tb, LANE), F32)],
        compiler_params=_cparams(("arbitrary", "arbitrary")),
        name="hgrn",
    )(q, lf2, v, q, lf2, v, s0, ind, bm)


def _hgrn_consts():
    c = np.arange(HG_W)
    blk = (c[:, None] // HEAD_DIM) == (c[None, :] // HEAD_DIM)
    return jnp.asarray(blk.astype(np.float32), BF16), jnp.asarray(blk.astype(np.float32))


def _half_masks(rows):
    lane = lax.broadcasted_iota(jnp.int32, (rows, LANE), 1)
    return lane < HEAD_DIM, lane >= HEAD_DIM


def _sel(mask, a):
    return jnp.where(mask, a, jnp.zeros_like(a))


def _value_slabs(v):
    lane = lax.broadcasted_iota(jnp.int32, v.shape, 1)
    lo, hi = _half_masks(v.shape[0])
    e_lo = jnp.where(lane == HEAD_DIM, 1.0, 0.0).astype(v.dtype)
    e_hi = jnp.where(lane == 0, 1.0, 0.0).astype(v.dtype)
    return _sel(lo, v) + e_lo, _sel(hi, v) + e_hi


def _normalize_slab(a_lo, a_hi):
    lane = lax.broadcasted_iota(jnp.int32, a_lo.shape, 1)
    return jnp.where(lane < HEAD_DIM, a_lo / a_lo[:, HEAD_DIM:HEAD_DIM + 1], a_hi / a_hi[:, 0:1])


def _gqa_kernel(tq, tk, n_lat, q_ref, kl_ref, vtl_ref, kc_ref, vtc_ref, o_ref, kmax_scr, m_scr, acc_scr):
    i = pl.program_id(1)
    nslab = GQA_W // LANE
    n_chunks = n_lat // tk
    heads = [(s, half) for s in range(nslab) for half in range(2)]

    def key_chunk(c):
        start = pl.multiple_of(c * tk, tk)
        return ([kl_ref[0, h, pl.ds(start, tk), :] for h in range(2)],
                [vtl_ref[0, h, :, pl.ds(start, tk)] for h in range(2)])

    @pl.when(i == 0)
    def _():
        for h in range(2):
            def norm2(kb):
                x = kb.astype(F32)
                lane = lax.broadcasted_iota(jnp.int32, x.shape, 1)
                x = jnp.where((lane < HEAD_DIM) if h == 0 else (lane >= HEAD_DIM), x, 0.0)
                mx = jnp.max(jnp.sum(x * x, axis=1, keepdims=True), axis=0, keepdims=True)
                return jnp.broadcast_to(mx, kmax_scr.shape[1:])

            mx = lax.fori_loop(0, n_chunks, lambda c, m: jnp.maximum(m, norm2(key_chunk(c)[0][h])),
                               norm2(kc_ref[0, h]))
            kmax_scr[h] = jnp.sqrt(mx)

    qlo, qhi = _half_masks(tq)
    lane = lax.broadcasted_iota(jnp.int32, (tq, LANE), 1)
    q_all = q_ref[0].astype(F32)
    rr = lax.broadcasted_iota(jnp.int32, (GQA_W, GQA_W), 0) // HEAD_DIM
    cc = lax.broadcasted_iota(jnp.int32, (GQA_W, GQA_W), 1) // HEAD_DIM
    qn2 = _dot((q_all * q_all).astype(BF16), jnp.where(rr == cc, 1.0, 0.0).astype(BF16))
    kmax = jnp.where(lane[0:1] < HEAD_DIM, kmax_scr[0][0:1], kmax_scr[1][0:1])
    bound = jnp.sqrt(qn2) * jnp.concatenate([kmax] * nslab, axis=1)
    use_bound = 2.0 * jnp.max(bound) < GQA_SHIFT_LIMIT

    def query(idx, bound_path):
        s, half = heads[idx]
        qf = q_all[:, s * LANE:(s + 1) * LANE]
        if not bound_path:
            return jnp.where(qhi if half else qlo, qf, 0.0).astype(BF16)
        swapped = pltpu.roll(bound[:, s * LANE:(s + 1) * LANE], HEAD_DIM, 1)
        return jnp.where(lane == (0 if half else HEAD_DIM), -swapped, qf).astype(BF16)

    qshift = [query(idx, True) for idx in range(len(heads))]

    def step_bound(k2, vt):
        sts = [_dot_nt(k2[half], qshift[idx]) for idx, (s, half) in enumerate(heads)]
        for idx, (s, half) in enumerate(heads):
            acc_scr[idx] = acc_scr[idx] + _dot(vt[half], jnp.exp2(sts[idx]).astype(BF16))

    def step_exact(qh, k2, vt):
        sts = [_dot_nt(k2[half], qh[idx]) for idx, (s, half) in enumerate(heads)]
        for idx, (s, half) in enumerate(heads):
            m_old = m_scr[idx]
            m_new = jnp.maximum(m_old, jnp.max(sts[idx], axis=0, keepdims=True))
            pt = jnp.exp2(sts[idx] - m_new).astype(BF16)
            acc_scr[idx] = jnp.exp2(m_old - m_new) * acc_scr[idx] + _dot(vt[half], pt)
            m_scr[idx] = m_new

    def run(step):
        def body(c, carry):
            step(*key_chunk(c))
            return carry

        lax.fori_loop(0, n_chunks, body, 0)
        step([kc_ref[0, h] for h in range(2)], [vtc_ref[0, h] for h in range(2)])

    acc_scr[...] = jnp.zeros(acc_scr.shape, F32)

    @pl.when(use_bound)
    def _():
        run(step_bound)

    @pl.when(jnp.logical_not(use_bound))
    def _():
        m_scr[...] = jnp.full(m_scr.shape, -jnp.inf, F32)
        run(functools.partial(step_exact, [query(idx, False) for idx in range(len(heads))]))

    for s in range(nslab):
        halves = [acc_scr[2 * s + half] for half in range(2)]
        out_t = jnp.concatenate([a[:HEAD_DIM] / a[HEAD_DIM:HEAD_DIM + 1] for a in halves], axis=0)
        o_ref[0, :, s * LANE:(s + 1) * LANE] = out_t.T.astype(BF16)


def _gqa(q, k_lat, vt_lat, k_ctx, vt_ctx, tq, tk):
    b, t, _ = q.shape
    n_lat = k_lat.shape[2]
    n_ctx = k_ctx.shape[2]
    full4 = lambda bi, i: (bi, 0, 0, 0)
    return pl.pallas_call(
        functools.partial(_gqa_kernel, tq, tk, n_lat),
        grid=(b, t // tq),
        in_specs=[
            pl.BlockSpec((1, tq, GQA_W), lambda bi, i: (bi, i, 0)),
            pl.BlockSpec((1, 2, n_lat, LANE), full4),
            pl.BlockSpec((1, 2, GQA_VT_ROWS, n_lat), full4),
            pl.BlockSpec((1, 2, n_ctx, LANE), full4),
            pl.BlockSpec((1, 2, GQA_VT_ROWS, n_ctx), full4),
        ],
        out_specs=pl.BlockSpec((1, tq, GQA_W), lambda bi, i: (bi, i, 0)),
        out_shape=jax.ShapeDtypeStruct((b, t, GQA_W), BF16),
        scratch_shapes=[
            pltpu.VMEM((GQA_KV_HEADS, 8, LANE), F32),
            pltpu.VMEM((GQA_HEADS, 1, tq), F32),
            pltpu.VMEM((GQA_HEADS, GQA_VT_ROWS, tq), F32),
        ],
        compiler_params=_cparams(("arbitrary", "arbitrary")),
        name="gqa",
    )(q, k_lat, vt_lat, k_ctx, vt_ctx)


def _ctxattn_kernel(nslab, shared_kv, q_ref, k_ref, v_ref, o_ref):
    n = q_ref.shape[1]
    lo, hi = _half_masks(n)
    for s in range(nslab):
        ks = 0 if shared_kv else s
        q = q_ref[0, :, s * LANE:(s + 1) * LANE]
        k = k_ref[0, :, ks * LANE:(ks + 1) * LANE]
        v = v_ref[0, :, ks * LANE:(ks + 1) * LANE]
        out = None
        for mask in (lo, hi):
            sc = _dot_nt(_sel(mask, q), k)
            p = jnp.exp2(sc - jnp.max(sc, axis=1, keepdims=True))
            o_h = _dot(p.astype(BF16), _sel(mask, v)) / jnp.sum(p, axis=1, keepdims=True)
            out = o_h if out is None else out + o_h
        o_ref[0, :, s * LANE:(s + 1) * LANE] = out.astype(BF16)


def _ctxattn(q, k, v, shared_kv):
    b, n, w = q.shape
    kw = k.shape[2]
    full = lambda bi: (bi, 0, 0)
    return pl.pallas_call(
        functools.partial(_ctxattn_kernel, w // LANE, shared_kv),
        grid=(b,),
        in_specs=[pl.BlockSpec((1, n, w), full), pl.BlockSpec((1, n, kw), full), pl.BlockSpec((1, n, kw), full)],
        out_specs=pl.BlockSpec((1, n, w), full),
        out_shape=jax.ShapeDtypeStruct((b, n, w), BF16),
        compiler_params=_cparams(("arbitrary",)),
        name="ctxattn",
    )(q, k, v)


def _na_window_start(i, r, rows):
    return jnp.clip(i * r - NA_KH // 2, 0, rows - (r + NA_KH))


def _na_kernel(r, rows, q_ref, k_ref, v_ref, kc_ref, vc_ref, eb_ref, br_ref, o_ref, kmax_scr):
    i = pl.program_id(1)
    tq = r * GRID_W
    kwin = (r + NA_KH) * GRID_W
    n_lat = k_ref.shape[1]
    n_ctx = kc_ref.shape[1]
    heads = [(s, half) for s in range(NA_W // LANE) for half in range(2)]

    @pl.when(i == 0)
    def _():
        rr = lax.broadcasted_iota(jnp.int32, (NA_W, NA_W), 0) // HEAD_DIM
        cc = lax.broadcasted_iota(jnp.int32, (NA_W, NA_W), 1) // HEAD_DIM
        ones_bd = jnp.where(rr == cc, 1.0, 0.0).astype(BF16)

        def norm2(kb):
            x = kb.astype(F32)
            n2 = _dot((x * x).astype(BF16), ones_bd)
            return jnp.broadcast_to(jnp.max(n2, axis=0, keepdims=True), kmax_scr.shape)

        ck = min(1024, n_lat)
        mx = lax.fori_loop(0, n_lat // ck,
                           lambda c, m: jnp.maximum(m, norm2(k_ref[0, pl.ds(pl.multiple_of(c * ck, ck), ck), :])),
                           norm2(kc_ref[0]))
        kmax_scr[...] = jnp.sqrt(mx)

    start = pl.multiple_of(_na_window_start(i, r, rows) * GRID_W, GRID_W)
    lane = lax.broadcasted_iota(jnp.int32, (tq, LANE), 1)
    qlo, qhi = _half_masks(tq)

    def ones_lane(n, half, dtype):
        ln = lax.broadcasted_iota(jnp.int32, (n, LANE), 1)
        return jnp.where(ln == (0 if half else HEAD_DIM), 1.0, 0.0).astype(dtype)

    q_all = q_ref[0].astype(F32)
    rr = lax.broadcasted_iota(jnp.int32, (NA_W, NA_W), 0) // HEAD_DIM
    cc = lax.broadcasted_iota(jnp.int32, (NA_W, NA_W), 1) // HEAD_DIM
    qn2 = _dot((q_all * q_all).astype(BF16), jnp.where(rr == cc, 1.0, 0.0).astype(BF16))
    reach = jnp.sqrt(qn2) * kmax_scr[0:1, :]
    shift = reach + br_ref[0][0:1, :]
    use_bound = jnp.max(2.0 * reach + (br_ref[0][0:1, :] - br_ref[1][0:1, :])) < GQA_SHIFT_LIMIT
    def query(h, bound_path):
        s, half = heads[h]
        qf = q_all[:, s * LANE:(s + 1) * LANE]
        if not bound_path:
            return jnp.where(qhi if half else qlo, qf, 0.0).astype(BF16)
        swapped = pltpu.roll(shift[:, s * LANE:(s + 1) * LANE], HEAD_DIM, 1)
        return jnp.where(lane == (0 if half else HEAD_DIM), -swapped, qf).astype(BF16)

    def attend(bound_path):
        for s in range(NA_W // LANE):
            sl = slice(s * LANE, (s + 1) * LANE)
            kw = k_ref[0, pl.ds(start, kwin), sl]
            kc = kc_ref[0, :, sl]
            vws = _value_slabs(v_ref[0, pl.ds(start, kwin), sl])
            vcs = _value_slabs(vc_ref[0, :, sl])
            wm, cm = _half_masks(kwin), _half_masks(n_ctx)
            outs = []
            for half in range(2):
                h = 2 * s + half
                qv = query(h, bound_path)
                if bound_path:
                    s_loc = _dot_nt(qv, _sel(wm[half], kw) + ones_lane(kwin, half, BF16)) + eb_ref[0, h]
                    s_ctx = _dot_nt(qv, _sel(cm[half], kc) + ones_lane(n_ctx, half, BF16))
                else:
                    s_loc = _dot_nt(qv, kw) + eb_ref[0, h]
                    s_ctx = _dot_nt(qv, kc)
                    m = jnp.maximum(jnp.max(s_loc, axis=1, keepdims=True), jnp.max(s_ctx, axis=1, keepdims=True))
                    s_loc, s_ctx = s_loc - m, s_ctx - m
                outs.append(_dot(jnp.exp2(s_loc).astype(BF16), vws[half])
                            + _dot(jnp.exp2(s_ctx).astype(BF16), vcs[half]))
            o_ref[0, :, sl] = _normalize_slab(outs[0], outs[1]).astype(BF16)

    @pl.when(use_bound)
    def _():
        attend(True)

    @pl.when(jnp.logical_not(use_bound))
    def _():
        attend(False)


def _na(layer, q, k, v, k_ctx, v_ctx, eb, brange, r):
    b, t, w = q.shape
    rows = t // GRID_W
    nblk = rows // r
    tq = r * GRID_W
    kwin = (r + NA_KH) * GRID_W
    n_ctx = k_ctx.shape[1]
    full = lambda bi, i: (bi, 0, 0)

    def variant(bi, i):
        return (jnp.where(i == 0, 0, jnp.where(i == nblk - 1, 2, 1)), layer, 0, 0)

    return pl.pallas_call(
        functools.partial(_na_kernel, r, rows),
        grid=(b, nblk),
        in_specs=[
            pl.BlockSpec((1, tq, w), lambda bi, i: (bi, i, 0)),
            pl.BlockSpec((1, t, w), full),
            pl.BlockSpec((1, t, w), full),
            pl.BlockSpec((1, n_ctx, w), full),
            pl.BlockSpec((1, n_ctx, w), full),
            pl.BlockSpec((1, NA_HEADS, tq, kwin), variant),
            pl.BlockSpec((2, 8, NA_W), lambda bi, i: (0, 0, 0)),
        ],
        out_specs=pl.BlockSpec((1, tq, w), lambda bi, i: (bi, i, 0)),
        out_shape=jax.ShapeDtypeStruct((b, t, w), BF16),
        scratch_shapes=[pltpu.VMEM((8, NA_W), F32)],
        compiler_params=_cparams(("arbitrary", "arbitrary")),
        name="na",
    )(q, k, v, k_ctx, v_ctx, eb, brange)


def _na_bias_tables(rpb, r, rows):
    nblk = rows // r
    nkr = r + NA_KH
    qc = np.arange(GRID_W)
    kc = np.arange(GRID_W)
    cs = np.clip(qc - NA_KW // 2, 0, GRID_W - NA_KW)
    col_ok = (kc[None, :] >= cs[:, None]) & (kc[None, :] < cs[:, None] + NA_KW)
    dc = kc[None, :] - qc[:, None] + NA_KW - 1
    col_oh = (dc[None] == np.arange(2 * NA_KW - 1)[:, None, None]) & col_ok[None]
    t1 = jnp.einsum("hij,jqk->hiqk", rpb, jnp.asarray(col_oh.astype(np.float32)), precision=lax.Precision.HIGHEST)
    t1 = t1 + jnp.asarray(np.where(col_ok, 0.0, NEG).astype(np.float32))
    neg_blk = jnp.full((rpb.shape[0], GRID_W, GRID_W), NEG, F32)
    variants = []
    for blk in (0, min(1, nblk - 1), nblk - 1):
        r0 = blk * r
        lo = int(np.clip(r0 - NA_KH // 2, 0, rows - nkr))
        rows_out = []
        for a in range(r):
            qr = r0 + a
            rs = int(np.clip(qr - NA_KH // 2, 0, rows - NA_KH))
            blks = []
            for bk in range(nkr):
                kr = lo + bk
                blks.append(t1[:, kr - qr + NA_KH - 1] if rs <= kr < rs + NA_KH else neg_blk)
            rows_out.append(jnp.concatenate(blks, axis=-1))
        variants.append(jnp.concatenate(rows_out, axis=1))
    return jnp.stack(variants)


def _layer_norm(x, g, b):
    xc = x - jnp.mean(x, axis=-1, keepdims=True)
    return xc * lax.rsqrt(jnp.mean(xc * xc, axis=-1, keepdims=True) + NORM_EPS) * g + b


def _post_kernel(hc, x_ref, of_ref, ob_ref, sg_ref, gq_ref, na_ref, mod_ref, hn_ref, bd_ref, wo_ref, w1_ref, w2_ref,
                 ln_ref, o_ref):
    g1, sh2, sc2, g2 = (mod_ref[0, k:k + 1, :] for k in range(4))
    tm = x_ref.shape[1]
    parts = [slice(p * (tm // POST_PARTS), (p + 1) * (tm // POST_PARTS)) for p in range(POST_PARTS)]
    mix = []
    for rs in parts:
        o = jnp.concatenate([of_ref[0, c, rs, :] + ob_ref[0, c, rs, :] for c in range(HG_W // LANE)], axis=1)
        ms = _dot((o * o).astype(BF16), bd_ref[...])
        hg = (o * lax.rsqrt(ms + NORM_EPS) * hn_ref[...] * sg_ref[0, rs, :]).astype(BF16)
        mix.append(jnp.concatenate([hg, gq_ref[0, rs, :], na_ref[0, rs, :]], axis=1))
    ys = [_dot(m, wo_ref[...]) for m in mix]
    x1s = [_layer_norm(DN_ALPHA * x_ref[0, rs, :] + g1 * y, ln_ref[0:1, :], ln_ref[1:2, :]) for rs, y in zip(parts, ys)]
    h2s = [(x1 * (1.0 + sc2) + sh2).astype(BF16) for x1 in x1s]
    fs, c0 = [None] * POST_PARTS, 0
    for width in hc:
        ugs = [_dot(h2, w1_ref[:, c0:c0 + width]) for h2 in h2s]
        uus = [_dot(h2, w1_ref[:, FFN_HIDDEN + c0:FFN_HIDDEN + c0 + width]) for h2 in h2s]
        for p in range(POST_PARTS):
            a = (ugs[p] * _sigmoid(ugs[p]) * uus[p]).astype(BF16)
            part = _dot(a, w2_ref[c0:c0 + width, :])
            fs[p] = part if fs[p] is None else fs[p] + part
        c0 += width
    for rs, x1, f in zip(parts, x1s, fs):
        o_ref[0, rs, :] = _layer_norm(DN_ALPHA * x1 + g2 * f, ln_ref[2:3, :], ln_ref[3:4, :])


def _post(layer, x, o_f, o_b, sg, gq, na, mod4, hn, bd256, w_out, w1, w2, ln4, tm, hc):
    b, t, d = x.shape
    row = lambda bi, i: (bi, i, 0)
    const2 = lambda bi, i: (0, 0)
    per_layer = lambda bi, i: (layer, 0, 0)
    return pl.pallas_call(
        functools.partial(_post_kernel, hc),
        grid=(b, t // tm),
        in_specs=[
            pl.BlockSpec((1, tm, d), row),
            pl.BlockSpec((1, HG_W // LANE, tm, LANE), lambda bi, i: (bi, 0, i, 0)),
            pl.BlockSpec((1, HG_W // LANE, tm, LANE), lambda bi, i: (bi, 0, i, 0)),
            pl.BlockSpec((1, tm, HG_W), row),
            pl.BlockSpec((1, tm, GQA_W), row),
            pl.BlockSpec((1, tm, NA_W), row),
            pl.BlockSpec((1, 4, d), lambda bi, i: (bi, 0, 0)),
            pl.BlockSpec((1, HG_W), const2),
            pl.BlockSpec((HG_W, HG_W), const2),
            pl.BlockSpec((None, d, d), per_layer),
            pl.BlockSpec((None, d, 2 * FFN_HIDDEN), per_layer),
            pl.BlockSpec((None, FFN_HIDDEN, d), per_layer),
            pl.BlockSpec((4, d), const2),
        ],
        out_specs=pl.BlockSpec((1, tm, d), row),
        out_shape=jax.ShapeDtypeStruct((b, t, d), F32),
        compiler_params=_cparams(("arbitrary", "arbitrary")),
        name="post",
    )(x, o_f, o_b, sg, gq, na, mod4, hn, bd256, w_out, w1, w2, ln4)


_PERM_MAIN = np.concatenate([np.arange(0, HEAD_DIM, 2), np.arange(1, HEAD_DIM, 2)])
_PERM_PART = np.concatenate([np.arange(1, HEAD_DIM, 2), np.arange(0, HEAD_DIM, 2)])


def _w_in_split(w_in):
    w = w_in.astype(BF16)
    gq0 = 5 * HG_W
    gk0 = gq0 + GQA_W

    def head(base, h, odd_first):
        pairs = w[:, :, base + h * HEAD_DIM:base + (h + 1) * HEAD_DIM].reshape(w.shape[0], w.shape[1], HEAD_DIM // 2, 2)
        even, odd = pairs[..., 0], pairs[..., 1]
        return jnp.concatenate([odd, even] if odd_first else [even, odd], axis=-1)

    cols = [head(gq0, h, False) for h in GQA_SLAB_HEADS] + [head(gk0, h, False) for h in range(GQA_KV_HEADS)]
    cols += [head(gq0, h, True) for h in GQA_SLAB_HEADS] + [head(gk0, h, True) for h in range(GQA_KV_HEADS)]
    return w, jnp.concatenate(cols, axis=-1)


def _w_out_rows(w_out):
    g0 = HG_W
    blocks = [w_out[:, :g0]] + [w_out[:, g0 + h * HEAD_DIM:g0 + (h + 1) * HEAD_DIM] for h in GQA_SLAB_HEADS]
    return jnp.concatenate(blocks + [w_out[:, g0 + GQA_W:]], axis=1).astype(BF16)


def _rope_tables(n_tokens):
    t = np.arange(n_tokens)
    row = (t // GRID_W).astype(np.float32)
    col = (t % GRID_W).astype(np.float32)
    n_pairs = HEAD_DIM // 4
    inv_freq = np.exp(-math.log(ROPE_THETA) * np.arange(n_pairs, dtype=np.float32) / n_pairs).astype(np.float32)
    ang = np.concatenate([row[:, None] * inv_freq, col[:, None] * inv_freq], -1).astype(np.float32)
    ang = jnp.asarray(ang)
    c, s = jnp.cos(ang), jnp.sin(ang)
    cos = jnp.tile(jnp.concatenate([c, c], -1), (1, LANE // HEAD_DIM))
    sin = jnp.tile(jnp.concatenate([-s, s], -1), (1, LANE // HEAD_DIM))
    return cos, sin


def _block_diag_mean(width):
    c = np.arange(width)
    return jnp.asarray(((c[:, None] // HEAD_DIM) == (c[None, :] // HEAD_DIM)).astype(np.float32) / HEAD_DIM, BF16)


def kernel(x, c, ctx, c_ctx, w_ada, b_ada, w_in, hgrn_lb_logits, hgrn_norm, gqa_q_norm, gqa_k_norm, na_rpb, w_out,
           w_ffn_in, w_ffn_out, ln_g, ln_b):
    b, n, d = x.shape
    n_ctx = ctx.shape[1]
    rows = n // GRID_W
    tm = min(512, n)
    tm_ctx = min(256, n_ctx)
    tb = 256
    tq, tk = min(1024, n), min(1024, n)
    na_r = 4
    hc = (1536, FFN_HIDDEN - 1536)

    cv = jnp.zeros((MOD_ROWS, d), F32).at[:b].set(c).at[b].set(c_ctx)
    mods = _mods(cv, w_ada, b_ada)

    w_in_b, w_rot = _w_in_split(w_in)
    w_out_p = _w_out_rows(w_out)
    w1 = w_ffn_in.astype(BF16)
    w2 = w_ffn_out.astype(BF16)

    cos_l, sin_l = _rope_tables(n)
    cos_c = jnp.ones((n_ctx, LANE), F32)
    sin_c = jnp.zeros((n_ctx, LANE), F32)
    bd128 = _block_diag_mean(LANE)
    bd256 = _block_diag_mean(HG_W)
    hg_consts = _hgrn_consts()
    s_zero = jnp.zeros((b, 2, HG_W, HG_W), F32)

    rpb2_all = na_rpb * math.log2(math.e)
    eb_all = _na_bias_tables(rpb2_all.reshape((-1,) + na_rpb.shape[2:]), na_r, rows)

    x_lat, x_ctx = x, ctx
    for l in range(DEPTH):
        need_ctx = l < DEPTH - 1
        m = mods[l].reshape(MOD_ROWS, N_MOD, d)
        m_lat = m[:b]
        m_ctx = jnp.broadcast_to(m[b][None], (b, N_MOD, d))
        qscale = HEAD_DIM ** -0.5 * math.log2(math.e)
        gains = (
            (jnp.tile(gqa_q_norm[l][_PERM_MAIN], GQA_HEADS) * qscale)[None],
            (jnp.tile(gqa_q_norm[l][_PERM_PART], GQA_HEADS) * qscale)[None],
            jnp.tile(gqa_k_norm[l][_PERM_MAIN], GQA_KV_HEADS)[None],
            jnp.tile(gqa_k_norm[l][_PERM_PART], GQA_KV_HEADS)[None],
        )
        p_lat = _inproj(l, x_lat, m_lat[:, 0:1], m_lat[:, 1:2], w_in_b, w_rot, hgrn_lb_logits, cos_l, sin_l, gains, bd128, tm)
        p_ctx = _inproj(l, x_ctx, m_ctx[:, 0:1], m_ctx[:, 1:2], w_in_b, w_rot, hgrn_lb_logits, cos_c, sin_c, gains, bd128,
                        tm_ctx)
        hq_l, lf_l, hi_l, sg_l, gq_l, gk_l, gv_l, nq_l, nk_l, nv_l, gvt_l, gk2_l = p_lat
        hq_c, lf_c, hi_c, sg_c, gq_c, gk_c, gv_c, nq_c, nk_c, nv_c, gvt_c, gk2_c = p_ctx

        *o_c, s_c = _hgrn(hq_c, lf_c, hi_c, s_zero, hg_consts, min(tb, n_ctx))
        *o_l, _ = _hgrn(hq_l, lf_l, hi_l, s_c, hg_consts, tb)

        gqa_l = _gqa(gq_l, gk2_l, gvt_l, gk2_c, gvt_c, tq, tk)
        rpb2 = rpb2_all[l]
        b_hi = jnp.maximum(jnp.max(rpb2, axis=(1, 2)), 0.0)
        b_lo = jnp.minimum(jnp.min(rpb2, axis=(1, 2)), 0.0)
        brange = jnp.broadcast_to(jnp.repeat(jnp.stack([b_hi, b_lo]), HEAD_DIM, axis=1)[:, None, :], (2, 8, NA_W))
        na_l = _na(l, nq_l, nk_l, nv_l, nk_c, nv_c, eb_all, brange, na_r)

        hn = jnp.tile(hgrn_norm[l], HG_HEADS)[None]
        ln4 = jnp.stack([ln_g[l, 0], ln_b[l, 0], ln_g[l, 1], ln_b[l, 1]])
        if need_ctx:
            gqa_c = _ctxattn(gq_c, gk_c, gv_c, True)
            na_c = _ctxattn(nq_c, nk_c, nv_c, False)
            x_ctx = _post(l, x_ctx, o_c[0], o_c[1], sg_c, gqa_c, na_c, m_ctx[:, 2:6], hn, bd256, w_out_p, w1, w2,
                          ln4, tm_ctx, hc)
        x_lat = _post(l, x_lat, o_l[0], o_l[1], sg_l, gqa_l, na_l, m_lat[:, 2:6], hn, bd256, w_out_p, w1, w2,
                      ln4, tm, hc)
    return x_lat
```

```python
import functools
import math

import numpy as np
import jax
import jax.numpy as jnp
from jax import lax
from jax.experimental import pallas as pl
from jax.experimental.pallas import tpu as pltpu

F32 = jnp.float32
BF16 = jnp.bfloat16

D_MODEL = 1024
DEPTH = 2
GRID_W = 64
HEAD_DIM = 64
HG_HEADS = 4
GQA_HEADS = 6
GQA_KV_HEADS = 2
NA_HEADS = 6
NA_KH = 8
NA_KW = 16
ROPE_THETA = 10000.0
NORM_EPS = 1e-6
N_MOD = 6
HG_W = HG_HEADS * HEAD_DIM
GQA_W = GQA_HEADS * HEAD_DIM
GQA_KV_W = GQA_KV_HEADS * HEAD_DIM
NA_W = NA_HEADS * HEAD_DIM
FFN_HIDDEN = 2816
DN_ALPHA = (2 * DEPTH) ** 0.25

LANE = 128
SUB = 16
MOD_ROWS = 8
NEG = -1e30
POST_PARTS = 2
GQA_VT_ROWS = 80
GQA_SHIFT_LIMIT = 100.0
VMEM_LIMIT = 56 * 1024 * 1024

C_HQ, C_FF, C_FB, C_HI, C_HGATE = 0, 256, 512, 768, 1024
C_GV, C_NK = 1792, 2304
GQA_SLAB_HEADS = (0, 3, 1, 4, 2, 5)


def _cparams(sem, flags=None):
    return pltpu.CompilerParams(dimension_semantics=sem, vmem_limit_bytes=VMEM_LIMIT, flags=flags)


def _sigmoid(z):
    return 1.0 / (1.0 + jnp.exp(-z))


def _dot(a, b):
    return jnp.dot(a, b, preferred_element_type=F32)


def _dot_nt(a, b):
    return lax.dot_general(a, b, (((1,), (1,)), ((), ())), preferred_element_type=F32)


def _mods_kernel(cv_ref, w_ref, b_ref, o_ref):
    cv = cv_ref[...]
    a = cv * _sigmoid(cv)
    o_ref[0] = jnp.dot(a, w_ref[0], preferred_element_type=F32, precision=lax.Precision.HIGHEST) + b_ref[0]


def _mods(cv, w_ada, b_ada):
    depth, d, nm = w_ada.shape
    tn = 1024
    return pl.pallas_call(
        _mods_kernel,
        grid=(depth, nm // tn),
        in_specs=[
            pl.BlockSpec((MOD_ROWS, d), lambda l, j: (0, 0)),
            pl.BlockSpec((1, d, tn), lambda l, j: (l, 0, j)),
            pl.BlockSpec((1, 1, tn), lambda l, j: (l, 0, j)),
        ],
        out_specs=pl.BlockSpec((1, MOD_ROWS, tn), lambda l, j: (l, 0, j)),
        out_shape=jax.ShapeDtypeStruct((depth, MOD_ROWS, nm), F32),
        compiler_params=_cparams(("arbitrary", "arbitrary")),
        name="mods",
    )(cv, w_ada, b_ada.reshape(depth, 1, nm))


def _inproj_kernel(layer, x_ref, sh_ref, sc_ref, w_ref, wr_ref, lbl_ref, cos_ref, sin_ref, gq_ref, gqp_ref, gk_ref, gkp_ref,
                   bd_ref, hq_ref, lf_ref, hi_ref, hg_ref, gq_o, gk_o, gv_o, nq_o, nk_o, nv_o, gvt_o, gk2_o):
    x = x_ref[0]
    h = (x * (1.0 + sc_ref[0]) + sh_ref[0]).astype(BF16)

    def proj(c0, width):
        return _dot(h, w_ref[:, c0:c0 + width])

    logits = [lbl_ref[d] for d in range(DEPTH)]
    mx = functools.reduce(jnp.maximum, logits)
    es = [jnp.exp(v - mx) for v in logits]
    inv = 1.0 / functools.reduce(lambda a, b: a + b, es)
    ps = [e * inv for e in es]
    lb = jnp.clip(functools.reduce(lambda a, b: a + b, ps[:layer + 1]) - ps[0], 0.0, 1.0)

    def put_halves(ref, first, a):
        for c in range(HG_W // LANE):
            ref[0, first + c] = a[:, c * LANE:(c + 1) * LANE]

    q = proj(C_HQ, HG_W)
    put_halves(hq_ref, 0, q * _sigmoid(q))
    for d, c0 in enumerate((C_FF, C_FB)):
        z = proj(c0, HG_W)
        lbd = lb[d:d + 1, :]
        put_halves(lf_ref, d * (HG_W // LANE), jnp.log(lbd + (1.0 - lbd) * _sigmoid(z)))
    put_halves(hi_ref, 0, proj(C_HI, HG_W))
    g = proj(C_HGATE, HG_W)
    hg_ref[0] = g * _sigmoid(g)

    cos = cos_ref[...]
    sin = sin_ref[...]
    bd = bd_ref[...]

    rot_w = GQA_W + GQA_KV_W
    main = _dot(h, wr_ref[:, :rot_w])
    part = _dot(h, wr_ref[:, rot_w:])
    nq_slabs = GQA_W // LANE
    for s in range(rot_w // LANE):
        sl = slice(s * LANE, (s + 1) * LANE)
        pm, pp = main[:, sl], part[:, sl]
        ms = _dot((pm * pm).astype(BF16), bd)
        r = lax.rsqrt(ms + NORM_EPS)
        if s < nq_slabs:
            gm, gp, out_ref, osl = gq_ref[:, sl], gqp_ref[:, sl], gq_o, sl
        else:
            osl = slice((s - nq_slabs) * LANE, (s - nq_slabs + 1) * LANE)
            gm, gp, out_ref = gk_ref[:, osl], gkp_ref[:, osl], gk_o
        out_ref[0, :, osl] = ((pm * r * gm) * cos + (pp * r * gp) * sin).astype(BF16)

    gv_nq = proj(C_GV, GQA_KV_W + NA_W)
    gv = gv_nq[:, :GQA_KV_W]
    gv_o[0] = gv.astype(BF16)
    nq_o[0] = (gv_nq[:, GQA_KV_W:] * (HEAD_DIM ** -0.5 * math.log2(math.e))).astype(BF16)
    nk_nv = proj(C_NK, 2 * NA_W)
    nk_o[0] = nk_nv[:, :NA_W].astype(BF16)
    nv_o[0] = nk_nv[:, NA_W:].astype(BF16)
    lane = lax.broadcasted_iota(jnp.int32, gv.shape, 1)
    for h, vh in enumerate((gv, pltpu.roll(gv, HEAD_DIM, 1))):
        vt = jnp.where(lane < HEAD_DIM, vh, jnp.where(lane == HEAD_DIM, 1.0, 0.0)).T
        gvt_o[0, h] = vt[:GQA_VT_ROWS].astype(BF16)
    gk = gk_o[0].astype(F32)
    gk2_o[0, 0] = jnp.where(lane < HEAD_DIM, gk, jnp.where(lane == HEAD_DIM, 1.0, 0.0)).astype(BF16)
    gk2_o[0, 1] = jnp.where(lane >= HEAD_DIM, gk, jnp.where(lane == 0, 1.0, 0.0)).astype(BF16)


def _inproj(layer, x, shift, scale, w_in, w_rot, lb_logits, cos, sin, gains, bd128, tm):
    b, t, d = x.shape
    gq, gqp, gk, gkp = gains
    row = lambda bi, i: (bi, i, 0)
    const2 = lambda bi, i: (0, 0)
    halves = (HG_W // LANE, 2 * HG_W // LANE, HG_W // LANE)
    widths = (HG_W, GQA_W, GQA_KV_W, GQA_KV_W, NA_W, NA_W, NA_W)
    dtypes = (F32, BF16, BF16, BF16, BF16, BF16, BF16)
    return pl.pallas_call(
        functools.partial(_inproj_kernel, layer),
        grid=(b, t // tm),
        in_specs=[
            pl.BlockSpec((1, tm, d), row),
            pl.BlockSpec((1, 1, d), lambda bi, i: (bi, 0, 0)),
            pl.BlockSpec((1, 1, d), lambda bi, i: (bi, 0, 0)),
            pl.BlockSpec((None, d, w_in.shape[2]), lambda bi, i: (layer, 0, 0)),
            pl.BlockSpec((None, d, w_rot.shape[2]), lambda bi, i: (layer, 0, 0)),
            pl.BlockSpec((DEPTH, 2, HG_W), lambda bi, i: (0, 0, 0)),
            pl.BlockSpec((tm, LANE), lambda bi, i: (i, 0)),
            pl.BlockSpec((tm, LANE), lambda bi, i: (i, 0)),
            pl.BlockSpec((1, GQA_W), const2),
            pl.BlockSpec((1, GQA_W), const2),
            pl.BlockSpec((1, GQA_KV_W), const2),
            pl.BlockSpec((1, GQA_KV_W), const2),
            pl.BlockSpec((LANE, LANE), const2),
        ],
        out_specs=[pl.BlockSpec((1, nh, tm, LANE), lambda bi, i: (bi, 0, i, 0)) for nh in halves]
        + [pl.BlockSpec((1, tm, w), row) for w in widths]
        + [pl.BlockSpec((1, 2, GQA_VT_ROWS, tm), lambda bi, i: (bi, 0, 0, i)),
           pl.BlockSpec((1, 2, tm, LANE), lambda bi, i: (bi, 0, i, 0))],
        out_shape=[jax.ShapeDtypeStruct((b, nh, t, LANE), F32) for nh in halves]
        + [jax.ShapeDtypeStruct((b, t, w), dt) for w, dt in zip(widths, dtypes)]
        + [jax.ShapeDtypeStruct((b, 2, GQA_VT_ROWS, t), BF16), jax.ShapeDtypeStruct((b, 2, t, LANE), BF16)],
        compiler_params=_cparams(("arbitrary", "arbitrary")),
        name="inproj",
    )(x, shift, scale, w_in, w_rot, lb_logits, cos, sin, gq, gqp, gk, gkp, bd128)


def _hgrn_kernel(tb, qf_ref, lff_ref, vf_ref, qb_ref, lfb_ref, vb_ref, s0_ref, ind_ref, bm_ref,
                 of_ref, ob_ref, sfin_ref, st_scr, stb_scr, qs_scr, ks_scr):
    i = pl.program_id(1)
    nsc = tb // SUB

    @pl.when(i == 0)
    def _():
        st_scr[...] = s0_ref[0]
        stb_scr[...] = s0_ref[0].astype(BF16)

    ind = ind_ref[...]
    in_refs = ((qf_ref, lff_ref, vf_ref), (qb_ref, lfb_ref, vb_ref))
    o_refs = (of_ref, ob_ref)

    nh = HG_W // LANE

    def slab(tt):
        return pl.ds(tt, nsc, stride=SUB)

    def load(ref, lead, rows):
        return jnp.concatenate([ref[(*lead, c, rows, slice(None))] for c in range(nh)], axis=1)

    def store(ref, lead, rows, a, add=False):
        for c in range(nh):
            idx = (*lead, c, rows, slice(None))
            piece = a[:, c * LANE:(c + 1) * LANE]
            ref[idx] = ref[idx] + piece if add else piece

    orders = (list(range(SUB)), list(range(SUB - 1, -1, -1)))
    q, v, f, k, dec = [], [], [], [], []
    for d, (q_ref, lf_ref, v_ref) in enumerate(in_refs):
        lf = {tt: load(lf_ref, (0,), slab(tt)) for tt in orders[d]}
        q.append({tt: load(q_ref, (0,), slab(tt)) for tt in orders[d]})
        v.append({tt: load(v_ref, (0,), slab(tt)) for tt in orders[d]})
        f.append({tt: jnp.exp(lf[tt]) for tt in orders[d]})
        k.append({tt: 1.0 - f[d][tt] for tt in orders[d]})
        cum, run = {}, None
        for tt in orders[d]:
            run = lf[tt] if run is None else run + lf[tt]
            cum[tt] = run
        for tt in orders[d]:
            store(qs_scr, (d,), slab(tt), q[d][tt] * jnp.exp(cum[tt]))
            store(ks_scr, (d,), slab(tt), k[d][tt] * jnp.exp(run - cum[tt]))
        dec.append(jnp.exp(run))

    acc = [dict(), dict()]
    for a in range(SUB):
        for d in range(2):
            ss = orders[d][a]
            u = k[d][ss]
            es = []
            for tt in orders[d][a:]:
                if tt != ss:
                    u = u * f[d][tt]
                es.append((q[d][tt] * u).astype(BF16))
            att = _dot(jnp.concatenate(es, axis=0), ind)
            for n, tt in enumerate(orders[d][a:]):
                part = att[n * nsc:(n + 1) * nsc] * v[d][ss]
                acc[d][tt] = part if a == 0 else acc[d][tt] + part
    for d in range(2):
        for tt in orders[d]:
            store(o_refs[d], (0,), slab(tt), acc[d][tt])

    bm = bm_ref[...]
    for step in range(nsc):
        for d in range(2):
            j = nsc - 1 - step if d else step
            sl = slice(j * SUB, (j + 1) * SUB)
            store(o_refs[d], (0,), sl, _dot_nt(load(qs_scr, (d,), sl).astype(BF16), stb_scr[d]), add=True)
            kvt = lax.dot_general(load(in_refs[d][2], (0,), sl).astype(BF16), load(ks_scr, (d,), sl).astype(BF16),
                                  (((0,), (0,)), ((), ())), preferred_element_type=F32)
            for h in range(HG_HEADS):
                rs = slice(h * HEAD_DIM, (h + 1) * HEAD_DIM)
                ls = slice((h // 2) * LANE, (h // 2 + 1) * LANE)
                blk = st_scr[d, rs, ls] * dec[d][j:j + 1, ls] + kvt[rs, ls] * bm[rs, ls]
                st_scr[d, rs, ls] = blk
                stb_scr[d, rs, ls] = blk.astype(BF16)

    @pl.when(i == pl.num_programs(1) - 1)
    def _():
        sfin_ref[0] = st_scr[...]


def _hgrn(q, lf2, v, s0, consts, tb):
    b, nh, t, _ = q.shape
    nblk = t // tb
    ind, bm = consts
    fwd = lambda bi, i: (bi, 0, i, 0)
    bwd = lambda bi, i: (bi, 0, nblk - 1 - i, 0)
    bwd_lf = lambda bi, i: (bi, 1, nblk - 1 - i, 0)
    const2 = lambda bi, i: (0, 0)
    state = lambda bi, i: (bi, 0, 0, 0)
    blk = (1, nh, tb, LANE)
    return pl.pallas_call(
        functools.partial(_hgrn_kernel, tb),
        grid=(b, nblk),
        in_specs=[
            pl.BlockSpec(blk, fwd), pl.BlockSpec(blk, fwd), pl.BlockSpec(blk, fwd),
            pl.BlockSpec(blk, bwd), pl.BlockSpec(blk, bwd_lf), pl.BlockSpec(blk, bwd),
            pl.BlockSpec((1, 2, HG_W, HG_W), state),
            pl.BlockSpec((HG_W, HG_W), const2),
            pl.BlockSpec((HG_W, HG_W), const2),
        ],
        out_specs=[pl.BlockSpec(blk, fwd), pl.BlockSpec(blk, bwd), pl.BlockSpec((1, 2, HG_W, HG_W), state)],
        out_shape=[
            jax.ShapeDtypeStruct((b, nh, t, LANE), F32),
            jax.ShapeDtypeStruct((b, nh, t, LANE), F32),
            jax.ShapeDtypeStruct((b, 2, HG_W, HG_W), F32),
        ],
        scratch_shapes=[pltpu.VMEM((2, HG_W, HG_W), F32), pltpu.VMEM((2, HG_W, HG_W), BF16),
                        pltpu.VMEM((2, nh, tb, LANE), F32), pltpu.VMEM((2, nh, tb, LANE), F32)],
        compiler_params=_cparams(("arbitrary", "arbitrary")),
        name="hgrn",
    )(q, lf2, v, q, lf2, v, s0, ind, bm)


def _hgrn_consts():
    c = np.arange(HG_W)
    blk = (c[:, None] // HEAD_DIM) == (c[None, :] // HEAD_DIM)
    return jnp.asarray(blk.astype(np.float32), BF16), jnp.asarray(blk.astype(np.float32))


def _half_masks(rows):
    lane = lax.broadcasted_iota(jnp.int32, (rows, LANE), 1)
    return lane < HEAD_DIM, lane >= HEAD_DIM


def _sel(mask, a):
    return jnp.where(mask, a, jnp.zeros_like(a))


def _value_slabs(v):
    lane = lax.broadcasted_iota(jnp.int32, v.shape, 1)
    lo, hi = _half_masks(v.shape[0])
    e_lo = jnp.where(lane == HEAD_DIM, 1.0, 0.0).astype(v.dtype)
    e_hi = jnp.where(lane == 0, 1.0, 0.0).astype(v.dtype)
    return _sel(lo, v) + e_lo, _sel(hi, v) + e_hi


def _normalize_slab(a_lo, a_hi):
    lane = lax.broadcasted_iota(jnp.int32, a_lo.shape, 1)
    return jnp.where(lane < HEAD_DIM, a_lo / a_lo[:, HEAD_DIM:HEAD_DIM + 1], a_hi / a_hi[:, 0:1])


def _gqa_kernel(tq, tk, n_lat, q_ref, kl_ref, vtl_ref, kc_ref, vtc_ref, o_ref, kmax_scr, m_scr, acc_scr):
    i = pl.program_id(1)
    nslab = GQA_W // LANE
    n_chunks = n_lat // tk
    heads = [(s, half) for s in range(nslab) for half in range(2)]

    def key_chunk(c):
        start = pl.multiple_of(c * tk, tk)
        return ([kl_ref[0, h, pl.ds(start, tk), :] for h in range(2)],
                [vtl_ref[0, h, :, pl.ds(start, tk)] for h in range(2)])

    @pl.when(i == 0)
    def _():
        for h in range(2):
            def norm2(kb):
                x = kb.astype(F32)
                lane = lax.broadcasted_iota(jnp.int32, x.shape, 1)
                x = jnp.where((lane < HEAD_DIM) if h == 0 else (lane >= HEAD_DIM), x, 0.0)
                mx = jnp.max(jnp.sum(x * x, axis=1, keepdims=True), axis=0, keepdims=True)
                return jnp.broadcast_to(mx, kmax_scr.shape[1:])

            mx = lax.fori_loop(0, n_chunks, lambda c, m: jnp.maximum(m, norm2(key_chunk(c)[0][h])),
                               norm2(kc_ref[0, h]))
            kmax_scr[h] = jnp.sqrt(mx)

    qlo, qhi = _half_masks(tq)
    lane = lax.broadcasted_iota(jnp.int32, (tq, LANE), 1)
    q_all = q_ref[0].astype(F32)
    rr = lax.broadcasted_iota(jnp.int32, (GQA_W, GQA_W), 0) // HEAD_DIM
    cc = lax.broadcasted_iota(jnp.int32, (GQA_W, GQA_W), 1) // HEAD_DIM
    qn2 = _dot((q_all * q_all).astype(BF16), jnp.where(rr == cc, 1.0, 0.0).astype(BF16))
    kmax = jnp.where(lane[0:1] < HEAD_DIM, kmax_scr[0][0:1], kmax_scr[1][0:1])
    bound = jnp.sqrt(qn2) * jnp.concatenate([kmax] * nslab, axis=1)
    use_bound = 2.0 * jnp.max(bound) < GQA_SHIFT_LIMIT

    def query(idx, bound_path):
        s, half = heads[idx]
        qf = q_all[:, s * LANE:(s + 1) * LANE]
        if not bound_path:
            return jnp.where(qhi if half else qlo, qf, 0.0).astype(BF16)
        swapped = pltpu.roll(bound[:, s * LANE:(s + 1) * LANE], HEAD_DIM, 1)
        return jnp.where(lane == (0 if half else HEAD_DIM), -swapped, qf).astype(BF16)

    qshift = [query(idx, True) for idx in range(len(heads))]

    def step_bound(k2, vt):
        sts = [_dot_nt(k2[half], qshift[idx]) for idx, (s, half) in enumerate(heads)]
        for idx, (s, half) in enumerate(heads):
            acc_scr[idx] = acc_scr[idx] + _dot(vt[half], jnp.exp2(sts[idx]).astype(BF16))

    def step_exact(qh, k2, vt):
        sts = [_dot_nt(k2[half], qh[idx]) for idx, (s, half) in enumerate(heads)]
        for idx, (s, half) in enumerate(heads):
            m_old = m_scr[idx]
            m_new = jnp.maximum(m_old, jnp.max(sts[idx], axis=0, keepdims=True))
            pt = jnp.exp2(sts[idx] - m_new).astype(BF16)
            acc_scr[idx] = jnp.exp2(m_old - m_new) * acc_scr[idx] + _dot(vt[half], pt)
            m_scr[idx] = m_new

    def run(step):
        def body(c, carry):
            step(*key_chunk(c))
            return carry

        lax.fori_loop(0, n_chunks, body, 0)
        step([kc_ref[0, h] for h in range(2)], [vtc_ref[0, h] for h in range(2)])

    acc_scr[...] = jnp.zeros(acc_scr.shape, F32)

    @pl.when(use_bound)
    def _():
        run(step_bound)

    @pl.when(jnp.logical_not(use_bound))
    def _():
        m_scr[...] = jnp.full(m_scr.shape, -jnp.inf, F32)
        run(functools.partial(step_exact, [query(idx, False) for idx in range(len(heads))]))

    for s in range(nslab):
        halves = [acc_scr[2 * s + half] for half in range(2)]
        out_t = jnp.concatenate([a[:HEAD_DIM] / a[HEAD_DIM:HEAD_DIM + 1] for a in halves], axis=0)
        o_ref[0, :, s * LANE:(s + 1) * LANE] = out_t.T.astype(BF16)


def _gqa(q, k_lat, vt_lat, k_ctx, vt_ctx, tq, tk):
    b, t, _ = q.shape
    n_lat = k_lat.shape[2]
    n_ctx = k_ctx.shape[2]
    full4 = lambda bi, i: (bi, 0, 0, 0)
    return pl.pallas_call(
        functools.partial(_gqa_kernel, tq, tk, n_lat),
        grid=(b, t // tq),
        in_specs=[
            pl.BlockSpec((1, tq, GQA_W), lambda bi, i: (bi, i, 0)),
            pl.BlockSpec((1, 2, n_lat, LANE), full4),
            pl.BlockSpec((1, 2, GQA_VT_ROWS, n_lat), full4),
            pl.BlockSpec((1, 2, n_ctx, LANE), full4),
            pl.BlockSpec((1, 2, GQA_VT_ROWS, n_ctx), full4),
        ],
        out_specs=pl.BlockSpec((1, tq, GQA_W), lambda bi, i: (bi, i, 0)),
        out_shape=jax.ShapeDtypeStruct((b, t, GQA_W), BF16),
        scratch_shapes=[
            pltpu.VMEM((GQA_KV_HEADS, 8, LANE), F32),
            pltpu.VMEM((GQA_HEADS, 1, tq), F32),
            pltpu.VMEM((GQA_HEADS, GQA_VT_ROWS, tq), F32),
        ],
        compiler_params=_cparams(("arbitrary", "arbitrary")),
        name="gqa",
    )(q, k_lat, vt_lat, k_ctx, vt_ctx)


def _ctxattn_kernel(nslab, shared_kv, q_ref, k_ref, v_ref, o_ref):
    n = q_ref.shape[1]
    lo, hi = _half_masks(n)
    for s in range(nslab):
        ks = 0 if shared_kv else s
        q = q_ref[0, :, s * LANE:(s + 1) * LANE]
        k = k_ref[0, :, ks * LANE:(ks + 1) * LANE]
        v = v_ref[0, :, ks * LANE:(ks + 1) * LANE]
        out = None
        for mask in (lo, hi):
            sc = _dot_nt(_sel(mask, q), k)
            p = jnp.exp2(sc - jnp.max(sc, axis=1, keepdims=True))
            o_h = _dot(p.astype(BF16), _sel(mask, v)) / jnp.sum(p, axis=1, keepdims=True)
            out = o_h if out is None else out + o_h
        o_ref[0, :, s * LANE:(s + 1) * LANE] = out.astype(BF16)


def _ctxattn(q, k, v, shared_kv):
    b, n, w = q.shape
    kw = k.shape[2]
    full = lambda bi: (bi, 0, 0)
    return pl.pallas_call(
        functools.partial(_ctxattn_kernel, w // LANE, shared_kv),
        grid=(b,),
        in_specs=[pl.BlockSpec((1, n, w), full), pl.BlockSpec((1, n, kw), full), pl.BlockSpec((1, n, kw), full)],
        out_specs=pl.BlockSpec((1, n, w), full),
        out_shape=jax.ShapeDtypeStruct((b, n, w), BF16),
        compiler_params=_cparams(("arbitrary",)),
        name="ctxattn",
    )(q, k, v)


def _na_window_start(i, r, rows):
    return jnp.clip(i * r - NA_KH // 2, 0, rows - (r + NA_KH))


def _na_kernel(r, rows, q_ref, k_ref, v_ref, kc_ref, vc_ref, eb_ref, br_ref, o_ref, kmax_scr):
    i = pl.program_id(1)
    tq = r * GRID_W
    kwin = (r + NA_KH) * GRID_W
    n_lat = k_ref.shape[1]
    n_ctx = kc_ref.shape[1]
    heads = [(s, half) for s in range(NA_W // LANE) for half in range(2)]

    @pl.when(i == 0)
    def _():
        rr = lax.broadcasted_iota(jnp.int32, (NA_W, NA_W), 0) // HEAD_DIM
        cc = lax.broadcasted_iota(jnp.int32, (NA_W, NA_W), 1) // HEAD_DIM
        ones_bd = jnp.where(rr == cc, 1.0, 0.0).astype(BF16)

        def norm2(kb):
            x = kb.astype(F32)
            n2 = _dot((x * x).astype(BF16), ones_bd)
            return jnp.broadcast_to(jnp.max(n2, axis=0, keepdims=True), kmax_scr.shape)

        ck = min(1024, n_lat)
        mx = lax.fori_loop(0, n_lat // ck,
                           lambda c, m: jnp.maximum(m, norm2(k_ref[0, pl.ds(pl.multiple_of(c * ck, ck), ck), :])),
                           norm2(kc_ref[0]))
        kmax_scr[...] = jnp.sqrt(mx)

    start = pl.multiple_of(_na_window_start(i, r, rows) * GRID_W, GRID_W)
    lane = lax.broadcasted_iota(jnp.int32, (tq, LANE), 1)
    qlo, qhi = _half_masks(tq)

    def ones_lane(n, half, dtype):
        ln = lax.broadcasted_iota(jnp.int32, (n, LANE), 1)
        return jnp.where(ln == (0 if half else HEAD_DIM), 1.0, 0.0).astype(dtype)

    q_all = q_ref[0].astype(F32)
    rr = lax.broadcasted_iota(jnp.int32, (NA_W, NA_W), 0) // HEAD_DIM
    cc = lax.broadcasted_iota(jnp.int32, (NA_W, NA_W), 1) // HEAD_DIM
    qn2 = _dot((q_all * q_all).astype(BF16), jnp.where(rr == cc, 1.0, 0.0).astype(BF16))
    reach = jnp.sqrt(qn2) * kmax_scr[0:1, :]
    shift = reach + br_ref[0][0:1, :]
    use_bound = jnp.max(2.0 * reach + (br_ref[0][0:1, :] - br_ref[1][0:1, :])) < GQA_SHIFT_LIMIT
    def query(h, bound_path):
        s, half = heads[h]
        qf = q_all[:, s * LANE:(s + 1) * LANE]
        if not bound_path:
            return jnp.where(qhi if half else qlo, qf, 0.0).astype(BF16)
        swapped = pltpu.roll(shift[:, s * LANE:(s + 1) * LANE], HEAD_DIM, 1)
        return jnp.where(lane == (0 if half else HEAD_DIM), -swapped, qf).astype(BF16)

    def attend(bound_path):
        for s in range(NA_W // LANE):
            sl = slice(s * LANE, (s + 1) * LANE)
            kw = k_ref[0, pl.ds(start, kwin), sl]
            kc = kc_ref[0, :, sl]
            vws = _value_slabs(v_ref[0, pl.ds(start, kwin), sl])
            vcs = _value_slabs(vc_ref[0, :, sl])
            wm, cm = _half_masks(kwin), _half_masks(n_ctx)
            outs = []
            for half in range(2):
                h = 2 * s + half
                qv = query(h, bound_path)
                if bound_path:
                    s_loc = _dot_nt(qv, _sel(wm[half], kw) + ones_lane(kwin, half, BF16)) + eb_ref[0, h]
                    s_ctx = _dot_nt(qv, _sel(cm[half], kc) + ones_lane(n_ctx, half, BF16))
                else:
                    s_loc = _dot_nt(qv, kw) + eb_ref[0, h]
                    s_ctx = _dot_nt(qv, kc)
                    m = jnp.maximum(jnp.max(s_loc, axis=1, keepdims=True), jnp.max(s_ctx, axis=1, keepdims=True))
                    s_loc, s_ctx = s_loc - m, s_ctx - m
                outs.append(_dot(jnp.exp2(s_loc).astype(BF16), vws[half])
                            + _dot(jnp.exp2(s_ctx).astype(BF16), vcs[half]))
            o_ref[0, :, sl] = _normalize_slab(outs[0], outs[1]).astype(BF16)

    @pl.when(use_bound)
    def _():
        attend(True)

    @pl.when(jnp.logical_not(use_bound))
    def _():
        attend(False)


def _na(layer, q, k, v, k_ctx, v_ctx, eb, brange, r):
    b, t, w = q.shape
    rows = t // GRID_W
    nblk = rows // r
    tq = r * GRID_W
    kwin = (r + NA_KH) * GRID_W
    n_ctx = k_ctx.shape[1]
    full = lambda bi, i: (bi, 0, 0)

    def variant(bi, i):
        return (jnp.where(i == 0, 0, jnp.where(i == nblk - 1, 2, 1)), layer, 0, 0)

    return pl.pallas_call(
        functools.partial(_na_kernel, r, rows),
        grid=(b, nblk),
        in_specs=[
            pl.BlockSpec((1, tq, w), lambda bi, i: (bi, i, 0)),
            pl.BlockSpec((1, t, w), full),
            pl.BlockSpec((1, t, w), full),
            pl.BlockSpec((1, n_ctx, w), full),
            pl.BlockSpec((1, n_ctx, w), full),
            pl.BlockSpec((1, NA_HEADS, tq, kwin), variant),
            pl.BlockSpec((2, 8, NA_W), lambda bi, i: (0, 0, 0)),
        ],
        out_specs=pl.BlockSpec((1, tq, w), lambda bi, i: (bi, i, 0)),
        out_shape=jax.ShapeDtypeStruct((b, t, w), BF16),
        scratch_shapes=[pltpu.VMEM((8, NA_W), F32)],
        compiler_params=_cparams(("arbitrary", "arbitrary")),
        name="na",
    )(q, k, v, k_ctx, v_ctx, eb, brange)


def _na_bias_tables(rpb, r, rows):
    nblk = rows // r
    nkr = r + NA_KH
    qc = np.arange(GRID_W)
    kc = np.arange(GRID_W)
    cs = np.clip(qc - NA_KW // 2, 0, GRID_W - NA_KW)
    col_ok = (kc[None, :] >= cs[:, None]) & (kc[None, :] < cs[:, None] + NA_KW)
    dc = kc[None, :] - qc[:, None] + NA_KW - 1
    col_oh = (dc[None] == np.arange(2 * NA_KW - 1)[:, None, None]) & col_ok[None]
    t1 = jnp.einsum("hij,jqk->hiqk", rpb, jnp.asarray(col_oh.astype(np.float32)), precision=lax.Precision.HIGHEST)
    t1 = t1 + jnp.asarray(np.where(col_ok, 0.0, NEG).astype(np.float32))
    neg_blk = jnp.full((rpb.shape[0], GRID_W, GRID_W), NEG, F32)
    variants = []
    for blk in (0, min(1, nblk - 1), nblk - 1):
        r0 = blk * r
        lo = int(np.clip(r0 - NA_KH // 2, 0, rows - nkr))
        rows_out = []
        for a in range(r):
            qr = r0 + a
            rs = int(np.clip(qr - NA_KH // 2, 0, rows - NA_KH))
            blks = []
            for bk in range(nkr):
                kr = lo + bk
                blks.append(t1[:, kr - qr + NA_KH - 1] if rs <= kr < rs + NA_KH else neg_blk)
            rows_out.append(jnp.concatenate(blks, axis=-1))
        variants.append(jnp.concatenate(rows_out, axis=1))
    return jnp.stack(variants)


def _layer_norm(x, g, b):
    xc = x - jnp.mean(x, axis=-1, keepdims=True)
    return xc * lax.rsqrt(jnp.mean(xc * xc, axis=-1, keepdims=True) + NORM_EPS) * g + b


def _post_kernel(hc, x_ref, of_ref, ob_ref, sg_ref, gq_ref, na_ref, mod_ref, hn_ref, bd_ref, wo_ref, w1_ref, w2_ref,
                 ln_ref, o_ref):
    g1, sh2, sc2, g2 = (mod_ref[0, k:k + 1, :] for k in range(4))
    tm = x_ref.shape[1]
    parts = [slice(p * (tm // POST_PARTS), (p + 1) * (tm // POST_PARTS)) for p in range(POST_PARTS)]
    mix = []
    for rs in parts:
        o = jnp.concatenate([of_ref[0, c, rs, :] + ob_ref[0, c, rs, :] for c in range(HG_W // LANE)], axis=1)
        ms = _dot((o * o).astype(BF16), bd_ref[...])
        hg = (o * lax.rsqrt(ms + NORM_EPS) * hn_ref[...] * sg_ref[0, rs, :]).astype(BF16)
        mix.append(jnp.concatenate([hg, gq_ref[0, rs, :], na_ref[0, rs, :]], axis=1))
    ys = [_dot(m, wo_ref[...]) for m in mix]
    x1s = [_layer_norm(DN_ALPHA * x_ref[0, rs, :] + g1 * y, ln_ref[0:1, :], ln_ref[1:2, :]) for rs, y in zip(parts, ys)]
    h2s = [(x1 * (1.0 + sc2) + sh2).astype(BF16) for x1 in x1s]
    fs, c0 = [None] * POST_PARTS, 0
    for width in hc:
        ugs = [_dot(h2, w1_ref[:, c0:c0 + width]) for h2 in h2s]
        uus = [_dot(h2, w1_ref[:, FFN_HIDDEN + c0:FFN_HIDDEN + c0 + width]) for h2 in h2s]
        for p in range(POST_PARTS):
            a = (ugs[p] * _sigmoid(ugs[p]) * uus[p]).astype(BF16)
            part = _dot(a, w2_ref[c0:c0 + width, :])
            fs[p] = part if fs[p] is None else fs[p] + part
        c0 += width
    for rs, x1, f in zip(parts, x1s, fs):
        o_ref[0, rs, :] = _layer_norm(DN_ALPHA * x1 + g2 * f, ln_ref[2:3, :], ln_ref[3:4, :])


def _post(layer, x, o_f, o_b, sg, gq, na, mod4, hn, bd256, w_out, w1, w2, ln4, tm, hc):
    b, t, d = x.shape
    row = lambda bi, i: (bi, i, 0)
    const2 = lambda bi, i: (0, 0)
    per_layer = lambda bi, i: (layer, 0, 0)
    return pl.pallas_call(
        functools.partial(_post_kernel, hc),
        grid=(b, t // tm),
        in_specs=[
            pl.BlockSpec((1, tm, d), row),
            pl.BlockSpec((1, HG_W // LANE, tm, LANE), lambda bi, i: (bi, 0, i, 0)),
            pl.BlockSpec((1, HG_W // LANE, tm, LANE), lambda bi, i: (bi, 0, i, 0)),
            pl.BlockSpec((1, tm, HG_W), row),
            pl.BlockSpec((1, tm, GQA_W), row),
            pl.BlockSpec((1, tm, NA_W), row),
            pl.BlockSpec((1, 4, d), lambda bi, i: (bi, 0, 0)),
            pl.BlockSpec((1, HG_W), const2),
            pl.BlockSpec((HG_W, HG_W), const2),
            pl.BlockSpec((None, d, d), per_layer),
            pl.BlockSpec((None, d, 2 * FFN_HIDDEN), per_layer),
            pl.BlockSpec((None, FFN_HIDDEN, d), per_layer),
            pl.BlockSpec((4, d), const2),
        ],
        out_specs=pl.BlockSpec((1, tm, d), row),
        out_shape=jax.ShapeDtypeStruct((b, t, d), F32),
        compiler_params=_cparams(("arbitrary", "arbitrary")),
        name="post",
    )(x, o_f, o_b, sg, gq, na, mod4, hn, bd256, w_out, w1, w2, ln4)


_PERM_MAIN = np.concatenate([np.arange(0, HEAD_DIM, 2), np.arange(1, HEAD_DIM, 2)])
_PERM_PART = np.concatenate([np.arange(1, HEAD_DIM, 2), np.arange(0, HEAD_DIM, 2)])


def _w_in_split(w_in):
    w = w_in.astype(BF16)
    gq0 = 5 * HG_W
    gk0 = gq0 + GQA_W

    def head(base, h, odd_first):
        pairs = w[:, :, base + h * HEAD_DIM:base + (h + 1) * HEAD_DIM].reshape(w.shape[0], w.shape[1], HEAD_DIM // 2, 2)
        even, odd = pairs[..., 0], pairs[..., 1]
        return jnp.concatenate([odd, even] if odd_first else [even, odd], axis=-1)

    cols = [head(gq0, h, False) for h in GQA_SLAB_HEADS] + [head(gk0, h, False) for h in range(GQA_KV_HEADS)]
    cols += [head(gq0, h, True) for h in GQA_SLAB_HEADS] + [head(gk0, h, True) for h in range(GQA_KV_HEADS)]
    return w, jnp.concatenate(cols, axis=-1)


def _w_out_rows(w_out):
    g0 = HG_W
    blocks = [w_out[:, :g0]] + [w_out[:, g0 + h * HEAD_DIM:g0 + (h + 1) * HEAD_DIM] for h in GQA_SLAB_HEADS]
    return jnp.concatenate(blocks + [w_out[:, g0 + GQA_W:]], axis=1).astype(BF16)


def _rope_tables(n_tokens):
    t = np.arange(n_tokens)
    row = (t // GRID_W).astype(np.float32)
    col = (t % GRID_W).astype(np.float32)
    n_pairs = HEAD_DIM // 4
    inv_freq = np.exp(-math.log(ROPE_THETA) * np.arange(n_pairs, dtype=np.float32) / n_pairs).astype(np.float32)
    ang = np.concatenate([row[:, None] * inv_freq, col[:, None] * inv_freq], -1).astype(np.float32)
    ang = jnp.asarray(ang)
    c, s = jnp.cos(ang), jnp.sin(ang)
    cos = jnp.tile(jnp.concatenate([c, c], -1), (1, LANE // HEAD_DIM))
    sin = jnp.tile(jnp.concatenate([-s, s], -1), (1, LANE // HEAD_DIM))
    return cos, sin


def _block_diag_mean(width):
    c = np.arange(width)
    return jnp.asarray(((c[:, None] // HEAD_DIM) == (c[None, :] // HEAD_DIM)).astype(np.float32) / HEAD_DIM, BF16)


def kernel(x, c, ctx, c_ctx, w_ada, b_ada, w_in, hgrn_lb_logits, hgrn_norm, gqa_q_norm, gqa_k_norm, na_rpb, w_out,
           w_ffn_in, w_ffn_out, ln_g, ln_b):
    b, n, d = x.shape
    n_ctx = ctx.shape[1]
    rows = n // GRID_W
    tm_in = min(1024, n)
    tm = min(512, n)
    tm_ctx = min(256, n_ctx)
    tb = 256
    tq, tk = min(1024, n), min(1024, n)
    na_r = 4
    hc = (1536, FFN_HIDDEN - 1536)

    cv = jnp.zeros((MOD_ROWS, d), F32).at[:b].set(c).at[b].set(c_ctx)
    mods = _mods(cv, w_ada, b_ada)

    w_in_b, w_rot = _w_in_split(w_in)
    w_out_p = _w_out_rows(w_out)
    w1 = w_ffn_in.astype(BF16)
    w2 = w_ffn_out.astype(BF16)

    cos_l, sin_l = _rope_tables(n)
    cos_c = jnp.ones((n_ctx, LANE), F32)
    sin_c = jnp.zeros((n_ctx, LANE), F32)
    bd128 = _block_diag_mean(LANE)
    bd256 = _block_diag_mean(HG_W)
    hg_consts = _hgrn_consts()
    s_zero = jnp.zeros((b, 2, HG_W, HG_W), F32)

    rpb2_all = na_rpb * math.log2(math.e)
    eb_all = _na_bias_tables(rpb2_all.reshape((-1,) + na_rpb.shape[2:]), na_r, rows)

    x_lat, x_ctx = x, ctx
    for l in range(DEPTH):
        need_ctx = l < DEPTH - 1
        m = mods[l].reshape(MOD_ROWS, N_MOD, d)
        m_lat = m[:b]
        m_ctx = jnp.broadcast_to(m[b][None], (b, N_MOD, d))
        qscale = HEAD_DIM ** -0.5 * math.log2(math.e)
        gains = (
            (jnp.tile(gqa_q_norm[l][_PERM_MAIN], GQA_HEADS) * qscale)[None],
            (jnp.tile(gqa_q_norm[l][_PERM_PART], GQA_HEADS) * qscale)[None],
            jnp.tile(gqa_k_norm[l][_PERM_MAIN], GQA_KV_HEADS)[None],
            jnp.tile(gqa_k_norm[l][_PERM_PART], GQA_KV_HEADS)[None],
        )
        p_lat = _inproj(l, x_lat, m_lat[:, 0:1], m_lat[:, 1:2], w_in_b, w_rot, hgrn_lb_logits, cos_l, sin_l, gains, bd128,
                        tm_in)
        p_ctx = _inproj(l, x_ctx, m_ctx[:, 0:1], m_ctx[:, 1:2], w_in_b, w_rot, hgrn_lb_logits, cos_c, sin_c, gains, bd128,
                        tm_ctx)
        hq_l, lf_l, hi_l, sg_l, gq_l, gk_l, gv_l, nq_l, nk_l, nv_l, gvt_l, gk2_l = p_lat
        hq_c, lf_c, hi_c, sg_c, gq_c, gk_c, gv_c, nq_c, nk_c, nv_c, gvt_c, gk2_c = p_ctx

        *o_c, s_c = _hgrn(hq_c, lf_c, hi_c, s_zero, hg_consts, min(tb, n_ctx))
        *o_l, _ = _hgrn(hq_l, lf_l, hi_l, s_c, hg_consts, tb)

        gqa_l = _gqa(gq_l, gk2_l, gvt_l, gk2_c, gvt_c, tq, tk)
        rpb2 = rpb2_all[l]
        b_hi = jnp.maximum(jnp.max(rpb2, axis=(1, 2)), 0.0)
        b_lo = jnp.minimum(jnp.min(rpb2, axis=(1, 2)), 0.0)
        brange = jnp.broadcast_to(jnp.repeat(jnp.stack([b_hi, b_lo]), HEAD_DIM, axis=1)[:, None, :], (2, 8, NA_W))
        na_l = _na(l, nq_l, nk_l, nv_l, nk_c, nv_c, eb_all, brange, na_r)

        hn = jnp.tile(hgrn_norm[l], HG_HEADS)[None]
        ln4 = jnp.stack([ln_g[l, 0], ln_b[l, 0], ln_g[l, 1], ln_b[l, 1]])
        if need_ctx:
            gqa_c = _ctxattn(gq_c, gk_c, gv_c, True)
            na_c = _ctxattn(nq_c, nk_c, nv_c, False)
            x_ctx = _post(l, x_ctx, o_c[0], o_c[1], sg_c, gqa_c, na_c, m_ctx[:, 2:6], hn, bd256, w_out_p, w1, w2,
                          ln4, tm_ctx, hc)
        x_lat = _post(l, x_lat, o_l[0], o_l[1], sg_l, gqa_l, na_l, m_lat[:, 2:6], hn, bd256, w_out_p, w1, w2,
                      ln4, tm, hc)
    return x_lat
```
